```python
import math
import jax, jax.numpy as jnp
from jax import lax
import numpy as np

D_MODEL = 1024
BATCH = 16
SEQ = 2048
DEPTH = 1

SSM_GROUP = 16
SSM_STATE = 64
SSM_WIDTH = D_MODEL // 2
SSM_GROUPS = SSM_WIDTH // SSM_GROUP
HEAD_DIM = 64
ATTN_PATTERNS = ((128, 1), (512, 4), (2048, 16))
N_PATTERNS = len(ATTN_PATTERNS)
HEADS_PER_PATTERN = (D_MODEL // 2) // HEAD_DIM
N_ATTN_HEADS = N_PATTERNS * HEADS_PER_PATTERN
ATTN_WIDTH = N_ATTN_HEADS * HEAD_DIM
ATTN_OUT = HEADS_PER_PATTERN * HEAD_DIM
BLOCK = 128
N_IN = SSM_WIDTH + 3 * ATTN_WIDTH + 2 * D_MODEL
D_FF = 2816
CONV_WIDTH = 3
EPS = 1e-6

kernel_name = "hybrid_s5_dilated_attn_convffn"


def rmsnorm(t, g):
    tf = t.astype(jnp.float32)
    return tf * lax.rsqrt(jnp.mean(tf * tf, axis=-1, keepdims=True) + EPS) * g.astype(jnp.float32)


def alibi_slopes():
    i = np.arange(1, N_ATTN_HEADS + 1, dtype=np.float32)
    s = np.exp2(-8.0 * i / N_ATTN_HEADS).astype(np.float32)
    return jnp.asarray(s.reshape(HEADS_PER_PATTERN, N_PATTERNS).T)


def s5_scan(u, lam_re, lam_im, log_dt, b_re, b_im, c_re, c_im, d_skip):
    bsz, l, _ = u.shape
    u = u.astype(jnp.float32).reshape(bsz, l, SSM_GROUPS, SSM_GROUP)
    dt = jnp.exp(log_dt.astype(jnp.float32))[:, None]
    lr = lam_re.astype(jnp.float32)
    li = lam_im.astype(jnp.float32)
    mag = jnp.exp(lr * dt)
    ar = mag * jnp.cos(li * dt)
    ai = mag * jnp.sin(li * dt)
    den = lr * lr + li * li
    cr = ((ar - 1.0) * lr + ai * li) / den
    ci = (ai * lr - (ar - 1.0) * li) / den
    br = b_re.astype(jnp.float32)
    bi = b_im.astype(jnp.float32)
    bbar_re = cr[..., None] * br - ci[..., None] * bi
    bbar_im = cr[..., None] * bi + ci[..., None] * br
    bu_re = jnp.einsum('blgc,gpc->lbgp', u, bbar_re)
    bu_im = jnp.einsum('blgc,gpc->lbgp', u, bbar_im)
    a_re = jnp.broadcast_to(ar, (l, 1, SSM_GROUPS, SSM_STATE))
    a_im = jnp.broadcast_to(ai, (l, 1, SSM_GROUPS, SSM_STATE))

    def combine(e1, e2):
        a1r, a1i, b1r, b1i = e1
        a2r, a2i, b2r, b2i = e2
        return (a2r * a1r - a2i * a1i,
                a2r * a1i + a2i * a1r,
                a2r * b1r - a2i * b1i + b2r,
                a2r * b1i + a2i * b1r + b2i)

    _, _, x_re, x_im = lax.associative_scan(combine, (a_re, a_im, bu_re, bu_im), axis=0)
    y = (jnp.einsum('gcp,lbgp->blgc', c_re.astype(jnp.float32), x_re)
         - jnp.einsum('gcp,lbgp->blgc', c_im.astype(jnp.float32), x_im))
    return y.reshape(bsz, l, SSM_WIDTH) + d_skip.astype(jnp.float32) * u.reshape(bsz, l, SSM_WIDTH)


def dilated_window_attention(q, k, v, slopes, window, dilation):
    b, l, h, dh = q.shape
    n = l // dilation
    reach = window // dilation
    nb = -(-n // BLOCK)
    n_pad = nb * BLOCK

    def strided(t):
        t = t.reshape(b, n, dilation, h, dh).transpose(0, 2, 1, 3, 4)
        t = jnp.pad(t, ((0, 0), (0, 0), (0, n_pad - n), (0, 0), (0, 0)))
        return t.reshape(b, dilation, nb, BLOCK, h, dh)

    def with_prev(t):
        prev = jnp.pad(t, ((0, 0), (0, 0), (1, 0), (0, 0), (0, 0), (0, 0)))[:, :, :-1]
        return jnp.concatenate([prev, t], axis=3)

    qb = strided(q)
    kc = with_prev(strided(k))
    vc = with_prev(strided(v)).astype(jnp.float32)
    s = jnp.einsum('brnqhd,brnkhd->brnhqk', qb, kc).astype(jnp.float32)
    qi = jnp.arange(BLOCK)[:, None]
    ki = jnp.arange(2 * BLOCK)[None, :]
    delta = qi + BLOCK - ki
    key_idx = jnp.arange(nb)[:, None] * BLOCK - BLOCK + jnp.arange(2 * BLOCK)[None, :]
    valid = ((delta >= 0) & (delta <= reach))[None] & (key_idx >= 0)[:, None, :]
    bias = -slopes.astype(jnp.float32)[:, None, None] * (delta * dilation).astype(jnp.float32)[None]
    s = jnp.where(valid[None, None, :, None], s + bias[None, None, None], -jnp.inf)
    m = jnp.max(s, axis=-1, keepdims=True)
    p = jnp.exp(s - m)
    den = jnp.sum(p, axis=-1, keepdims=True)
    o = jnp.einsum('brnhqk,brnkhd->brnqhd', p, vc) / jnp.swapaxes(den, 3, 4)
    lse = jnp.swapaxes((m + jnp.log(den))[..., 0], 3, 4)
    o = o.reshape(b, dilation, n_pad, h, dh)[:, :, :n].transpose(0, 2, 1, 3, 4).reshape(b, l, h, dh)
    lse = lse.reshape(b, dilation, n_pad, h)[:, :, :n].transpose(0, 2, 1, 3).reshape(b, l, h)
    return o, lse


def causal_dwconv(t, w, bias):
    f = t.shape[-1]
    y = lax.conv_general_dilated(
        t.astype(jnp.float32), w.astype(jnp.float32)[:, None, :],
        window_strides=(1,), padding=[(w.shape[0] - 1, 0)],
        dimension_numbers=('NWC', 'WIO', 'NWC'), feature_group_count=f)
    return y + bias.astype(jnp.float32)


def setup_inputs(seed: int = 0) -> dict:
    key = jax.random.key(seed)
    ks = jax.random.split(key, 24)
    f32 = jnp.float32

    def nrm(k, shape, scale):
        return jax.random.normal(k, shape, f32) * scale

    gp = (DEPTH, SSM_GROUPS, SSM_STATE)
    return {
        'x': nrm(ks[0], (BATCH, SEQ, D_MODEL), 1.0),
        'norm_mix_g': 1.0 + nrm(ks[1], (DEPTH, D_MODEL), 0.02),
        'w_in': nrm(ks[2], (DEPTH, D_MODEL, N_IN), D_MODEL ** -0.5),
        'ssm_lambda_re': -0.5 * jnp.exp(nrm(ks[3], gp, 0.05)),
        'ssm_lambda_im': jnp.broadcast_to(jnp.pi * jnp.arange(SSM_STATE, dtype=f32), gp) + nrm(ks[4], gp, 0.01),
        'ssm_log_dt': jax.random.uniform(ks[5], (DEPTH, SSM_GROUPS), f32, math.log(1e-3), math.log(1e-1)),
        'ssm_b_re': nrm(ks[6], (DEPTH, SSM_GROUPS, SSM_STATE, SSM_GROUP), (2 * SSM_GROUP) ** -0.5),
        'ssm_b_im': nrm(ks[7], (DEPTH, SSM_GROUPS, SSM_STATE, SSM_GROUP), (2 * SSM_GROUP) ** -0.5),
        'ssm_c_re': nrm(ks[8], (DEPTH, SSM_GROUPS, SSM_GROUP, SSM_STATE), SSM_STATE ** -0.5),
        'ssm_c_im': nrm(ks[9], (DEPTH, SSM_GROUPS, SSM_GROUP, SSM_STATE), SSM_STATE ** -0.5),
        'ssm_d': nrm(ks[10], (DEPTH, SSM_WIDTH), 1.0),
        'glu_w_val': nrm(ks[11], (DEPTH, SSM_WIDTH, D_MODEL), SSM_WIDTH ** -0.5),
        'glu_w_gate': nrm(ks[12], (DEPTH, SSM_WIDTH, D_MODEL), SSM_WIDTH ** -0.5),
        'q_norm_g': 1.0 + nrm(ks[13], (DEPTH, HEAD_DIM), 0.02),
        'k_norm_g': 1.0 + nrm(ks[14], (DEPTH, HEAD_DIM), 0.02),
        'w_attn_up': nrm(ks[15], (DEPTH, ATTN_OUT, D_MODEL), ATTN_OUT ** -0.5),
        'w_out': nrm(ks[16], (DEPTH, D_MODEL, D_MODEL), D_MODEL ** -0.5),
        'norm_ffn_g': 1.0 + nrm(ks[17], (DEPTH, D_MODEL), 0.02),
        'ffn_w_gate': nrm(ks[18], (DEPTH, D_MODEL, D_FF), D_MODEL ** -0.5),
        'ffn_w_up': nrm(ks[19], (DEPTH, D_MODEL, D_FF), D_MODEL ** -0.5),
        'ffn_conv_w': nrm(ks[20], (DEPTH, CONV_WIDTH, D_FF), CONV_WIDTH ** -0.5),
        'ffn_conv_b': nrm(ks[21], (DEPTH, D_FF), 0.01),
        'ffn_w_down': nrm(ks[22], (DEPTH, D_FF, D_MODEL), D_FF ** -0.5),
    }


def reference(x, norm_mix_g, w_in, ssm_lambda_re, ssm_lambda_im, ssm_log_dt, ssm_b_re, ssm_b_im,
              ssm_c_re, ssm_c_im, ssm_d, glu_w_val, glu_w_gate, q_norm_g, k_norm_g, w_attn_up,
              w_out, norm_ffn_g, ffn_w_gate, ffn_w_up, ffn_conv_w, ffn_conv_b, ffn_w_down):
    b, l, _ = x.shape
    slopes = alibi_slopes()
    scale = HEAD_DIM ** -0.5
    for i in range(DEPTH):
        hmix = rmsnorm(x, norm_mix_g[i])
        proj = hmix @ w_in[i].astype(jnp.float32)
        o1 = SSM_WIDTH
        o2 = o1 + 3 * ATTN_WIDTH
        u_ssm = proj[..., :o1]
        qkv = proj[..., o1:o2].reshape(b, l, 3, N_PATTERNS, HEADS_PER_PATTERN, HEAD_DIM)
        gate_a = jax.nn.sigmoid(proj[..., o2:o2 + D_MODEL])
        gate_b = jax.nn.sigmoid(proj[..., o2 + D_MODEL:])

        y_ssm = jax.nn.gelu(s5_scan(u_ssm, ssm_lambda_re[i], ssm_lambda_im[i], ssm_log_dt[i],
                                    ssm_b_re[i], ssm_b_im[i], ssm_c_re[i], ssm_c_im[i], ssm_d[i]),
                            approximate=False)
        y_a = (y_ssm @ glu_w_val[i].astype(jnp.float32)) * jax.nn.sigmoid(y_ssm @ glu_w_gate[i].astype(jnp.float32))

        q = rmsnorm(qkv[:, :, 0], q_norm_g[i]) * scale
        k = rmsnorm(qkv[:, :, 1], k_norm_g[i])
        v = qkv[:, :, 2]
        outs = []
        lses = []
        for g, (window, dilation) in enumerate(ATTN_PATTERNS):
            o_g, lse_g = dilated_window_attention(q[:, :, g], k[:, :, g], v[:, :, g], slopes[g], window, dilation)
            outs.append(o_g)
            lses.append(lse_g)
        wgt = jax.nn.softmax(jnp.stack(lses, axis=0), axis=0)
        o_attn = jnp.sum(wgt[..., None] * jnp.stack(outs, axis=0), axis=0).reshape(b, l, ATTN_OUT)
        y_b = o_attn @ w_attn_up[i].astype(jnp.float32)

        mixed = (gate_a * y_a + gate_b * y_b) @ w_out[i].astype(jnp.float32)
        x = (x.astype(jnp.float32) + mixed).astype(x.dtype)

        hf = rmsnorm(x, norm_ffn_g[i])
        g_pre = causal_dwconv(hf @ ffn_w_gate[i].astype(jnp.float32), ffn_conv_w[i], ffn_conv_b[i])
        ff = (jax.nn.gelu(g_pre, approximate=False) * (hf @ ffn_w_up[i].astype(jnp.float32))) @ ffn_w_down[i].astype(jnp.float32)
        x = (x.astype(jnp.float32) + ff).astype(x.dtype)
    return x
```

```python
import functools
import math

import numpy as np
import jax
import jax.numpy as jnp
from jax import lax
from jax.experimental import pallas as pl
from jax.experimental.pallas import tpu as pltpu

F32 = jnp.float32
BF16 = jnp.bfloat16

SSM_GROUP = 16
SSM_STATE = 64
HEAD_DIM = 64
ATTN_PATTERNS = ((128, 1), (512, 4), (2048, 16))
N_PATTERNS = len(ATTN_PATTERNS)
HEADS_PER_PATTERN = 8
N_ATTN_HEADS = N_PATTERNS * HEADS_PER_PATTERN
PATTERN_WIDTH = HEADS_PER_PATTERN * HEAD_DIM
BLOCK = 128
CONV_WIDTH = 3
EPS = 1e-6

LANES = 128
SUBLANES = 8
VMEM_LIMIT_BYTES = 56 * 1024 * 1024

IN_ROWS = 512
MIX_ROWS = 512
FFN_ROWS = 512
FFN_COLS = 256
SSM_STEPS = 32
GROUPS_PER_TILE = LANES // SSM_GROUP
PAIRS_PER_TILE = GROUPS_PER_TILE // 2
PAIR_STATE = 2 * SSM_STATE
PAIR_COLS = 2 * PAIR_STATE


def _resident(shape):
    nd = len(shape)
    return pl.BlockSpec(shape, lambda *_: (0,) * nd, pipeline_mode=pl.Buffered(1))


def _gelu(z):
    return 0.5 * z * (1.0 + lax.erf(z * (1.0 / math.sqrt(2.0))))


def _in_proj_kernel(x_ref, g_ref, w_ref, qg_ref, kg_ref, ones_ref,
                    u_ref, q0, q1, q2, k0, k1, k2, v0, v1, v2, ga_ref, gb_ref):
    x = x_ref[...]
    ms = jnp.mean(x * x, axis=-1, keepdims=True)
    h = (x * lax.rsqrt(ms + EPS) * g_ref[...]).astype(BF16)

    def proj(col0, width):
        return jnp.dot(h, w_ref[:, col0:col0 + width], preferred_element_type=F32)

    def head_norm(z, gain):
        sq = (z * z).astype(BF16)
        half = ones_ref.shape[0]
        parts = [jnp.dot(sq[:, c:c + half], ones_ref[...], preferred_element_type=F32)
                 for c in range(0, z.shape[1], half)]
        ssq = jnp.concatenate(parts, axis=1)
        return z * lax.rsqrt(ssq * (1.0 / HEAD_DIM) + EPS) * gain

    w = PATTERN_WIDTH
    u_ref[...] = proj(0, w)
    col = w
    for ref in (q0, q1, q2):
        ref[...] = head_norm(proj(col, w), qg_ref[...]).astype(BF16)
        col += w
    for ref in (k0, k1, k2):
        ref[...] = head_norm(proj(col, w), kg_ref[...]).astype(BF16)
        col += w
    for ref in (v0, v1, v2):
        ref[...] = proj(col, w).astype(BF16)
        col += w
    for ref in (ga_ref, gb_ref):
        for c in range(0, ref.shape[1], w):
            ref[:, c:c + w] = jax.nn.sigmoid(proj(col, w)).astype(BF16)
            col += w


def _in_proj(x2, norm_g, w_in, q_gain, k_gain):
    t, d = x2.shape
    n_in = w_in.shape[1]
    w = PATTERN_WIDTH
    d_gate = (n_in - w - 3 * N_PATTERNS * w) // 2
    ones_bd = jnp.asarray(np.kron(np.eye(256 // HEAD_DIM), np.ones((HEAD_DIM, HEAD_DIM))), BF16)
    row = lambda width: pl.BlockSpec((IN_ROWS, width), lambda i: (i, 0))
    out_shape = ([jax.ShapeDtypeStruct((t, w), F32)]
                 + [jax.ShapeDtypeStruct((t, w), BF16)] * 9
                 + [jax.ShapeDtypeStruct((t, d_gate), BF16)] * 2)
    return pl.pallas_call(
        _in_proj_kernel,
        grid=(t // IN_ROWS,),
        in_specs=[row(d), _resident((1, d)), _resident(w_in.shape),
                  _resident((1, w)), _resident((1, w)), _resident(ones_bd.shape)],
        out_specs=[row(w)] * 10 + [row(d_gate)] * 2,
        out_shape=out_shape,
        compiler_params=pltpu.CompilerParams(
            dimension_semantics=("arbitrary",), vmem_limit_bytes=VMEM_LIMIT_BYTES),
        name="in_proj",
    )(x2, norm_g, w_in, q_gain, k_gain, ones_bd)


def _ssm_kernel(u_ref, bp_ref, ar_ref, ai_ref, cp_ref, d_ref, y_ref, state_ref, bu_ref, xs_ref, *, batch):
    @pl.when(pl.program_id(0) == 0)
    def _():
        state_ref[...] = jnp.zeros_like(state_ref)

    n_tiles = u_ref.shape[1] // LANES
    for q in range(n_tiles):
        u_tile = u_ref[:, q * LANES:(q + 1) * LANES]
        ub = u_tile.astype(BF16)
        for gl in range(PAIRS_PER_TILE):
            gp = q * PAIRS_PER_TILE + gl
            bu_ref[:, gl * PAIR_COLS:(gl + 1) * PAIR_COLS] = jnp.dot(
                ub, bp_ref[gp], preferred_element_type=F32)

        ars = [ar_ref[q * PAIRS_PER_TILE + gl] for gl in range(PAIRS_PER_TILE)]
        ais = [ai_ref[q * PAIRS_PER_TILE + gl] for gl in range(PAIRS_PER_TILE)]

        def step(t, carry):
            rows = pl.ds(pl.multiple_of(t * batch, batch), batch)
            new = []
            for gl in range(PAIRS_PER_TILE):
                xr, xi = carry[gl]
                c0 = gl * PAIR_COLS
                br = bu_ref[rows, c0:c0 + PAIR_STATE]
                bi = bu_ref[rows, c0 + PAIR_STATE:c0 + PAIR_COLS]
                nr = ars[gl] * xr - ais[gl] * xi + br
                ni = ars[gl] * xi + ais[gl] * xr + bi
                xs_ref[rows, c0:c0 + PAIR_STATE] = nr
                xs_ref[rows, c0 + PAIR_STATE:c0 + PAIR_COLS] = ni
                new.append((nr, ni))
            return tuple(new)

        init = tuple((state_ref[q * PAIRS_PER_TILE + gl, 0], state_ref[q * PAIRS_PER_TILE + gl, 1])
                     for gl in range(PAIRS_PER_TILE))
        final = lax.fori_loop(0, SSM_STEPS, step, init, unroll=4)
        for gl in range(PAIRS_PER_TILE):
            state_ref[q * PAIRS_PER_TILE + gl, 0] = final[gl][0]
            state_ref[q * PAIRS_PER_TILE + gl, 1] = final[gl][1]

        y = jnp.dot(xs_ref[...].astype(BF16), cp_ref[q], preferred_element_type=F32)
        y = y + d_ref[:, q * LANES:(q + 1) * LANES] * u_tile
        y_ref[:, q * LANES:(q + 1) * LANES] = _gelu(y).astype(BF16)


def _ssm(u_tm, bpad, ar_v, ai_v, cpad, d_skip, batch):
    rows_total, width = u_tm.shape
    rows = SSM_STEPS * batch
    n_pairs = bpad.shape[0]
    kernel = functools.partial(_ssm_kernel, batch=batch)
    return pl.pallas_call(
        kernel,
        grid=(rows_total // rows,),
        in_specs=[pl.BlockSpec((rows, width), lambda i: (i, 0)),
                  _resident(bpad.shape), _resident(ar_v.shape), _resident(ai_v.shape),
                  _resident(cpad.shape), _resident(d_skip.shape)],
        out_specs=pl.BlockSpec((rows, width), lambda i: (i, 0)),
        out_shape=jax.ShapeDtypeStruct((rows_total, width), BF16),
        scratch_shapes=[pltpu.VMEM((n_pairs, 2, batch, PAIR_STATE), F32),
                        pltpu.VMEM((rows, PAIRS_PER_TILE * PAIR_COLS), F32),
                        pltpu.VMEM((rows, PAIRS_PER_TILE * PAIR_COLS), F32)],
        compiler_params=pltpu.CompilerParams(
            dimension_semantics=("arbitrary",), vmem_limit_bytes=VMEM_LIMIT_BYTES),
        name="ssm",
    )(u_tm, bpad, ar_v, ai_v, cpad, d_skip)


def _ssm_params(lam_re, lam_im, log_dt, b_re, b_im, c_re, c_im):
    g = lam_re.shape[0]
    dt = jnp.exp(log_dt.astype(F32))[:, None]
    lr = lam_re.astype(F32)
    li = lam_im.astype(F32)
    mag = jnp.exp(lr * dt)
    ar = mag * jnp.cos(li * dt)
    ai = mag * jnp.sin(li * dt)
    den = lr * lr + li * li
    cr = ((ar - 1.0) * lr + ai * li) / den
    ci = (ai * lr - (ar - 1.0) * li) / den
    br = b_re.astype(F32)
    bi = b_im.astype(F32)
    bbar_re = cr[..., None] * br - ci[..., None] * bi
    bbar_im = cr[..., None] * bi + ci[..., None] * br

    n_pairs = g // 2
    n_tiles = n_pairs // PAIRS_PER_TILE
    eye_pt = jnp.eye(PAIRS_PER_TILE, dtype=F32)
    eye_2 = jnp.eye(2, dtype=F32)
    bb = jnp.stack([bbar_re, bbar_im], axis=0).reshape(2, n_tiles, PAIRS_PER_TILE, 2, SSM_STATE, SSM_GROUP)
    bpad = jnp.einsum('aqlgpc,lm,gh->qlmgcahp', bb, eye_pt, eye_2)
    bpad = bpad.reshape(n_pairs, LANES, PAIR_COLS)
    cc = jnp.stack([c_re.astype(F32), -c_im.astype(F32)], axis=0)
    cc = cc.reshape(2, n_tiles, PAIRS_PER_TILE, 2, SSM_GROUP, SSM_STATE)
    cpad = jnp.einsum('aqlgcp,lm,gh->qlagpmhc', cc, eye_pt, eye_2)
    cpad = cpad.reshape(n_tiles, PAIRS_PER_TILE * PAIR_COLS, LANES)
    ar_v = ar.reshape(n_pairs, 1, PAIR_STATE)
    ai_v = ai.reshape(n_pairs, 1, PAIR_STATE)
    return bpad.astype(BF16), ar_v, ai_v, cpad.astype(BF16)


def _alibi_slopes():
    i = np.arange(1, N_ATTN_HEADS + 1, dtype=np.float32)
    s = np.exp2(-8.0 * i / N_ATTN_HEADS).astype(np.float32)
    return s.reshape(HEADS_PER_PATTERN, N_PATTERNS).T


def _attn_bias():
    slopes = _alibi_slopes()
    qi = np.arange(BLOCK)[:, None]
    ki = np.arange(2 * BLOCK)[None, :]
    delta = qi + BLOCK - ki
    out = np.empty((N_PATTERNS, HEADS_PER_PATTERN, BLOCK, 2 * BLOCK), np.float32)
    for g, (window, dilation) in enumerate(ATTN_PATTERNS):
        reach = window // dilation
        valid = (delta >= 0) & (delta <= reach)
        dist = (delta * dilation).astype(np.float32)
        for h in range(HEADS_PER_PATTERN):
            out[g, h] = np.where(valid, -slopes[g, h] * dist, -np.inf)
    return out


def _attn_kernel(q0, k0, v0, q1, k1, v1, q2, k2, v2, bias_ref, o_ref,
                 og0, og1, og2, lg0, lg1, lg2):
    qkv = ((q0, k0, v0), (q1, k1, v1), (q2, k2, v2))
    ogs = (og0, og1, og2)
    lgs = (lg0, lg1, lg2)
    seq = q0.shape[0]
    heads = LANES // HEAD_DIM
    lane = lax.broadcasted_iota(jnp.int32, (BLOCK, LANES), 1)

    def block(g, row0, has_prev, out_start):
        q_ref, k_ref, v_ref = qkv[g]
        dilation = ATTN_PATTERNS[g][1]
        q = q_ref[pl.ds(row0, BLOCK), :]
        if has_prev:
            keys = pl.ds(row0 - BLOCK, 2 * BLOCK)
        else:
            keys = pl.ds(row0, BLOCK)
        k = k_ref[keys, :]
        v = v_ref[keys, :]
        o = None
        lse = None
        for hh in range(heads):
            in_head = (lane >= hh * HEAD_DIM) & (lane < (hh + 1) * HEAD_DIM)
            qm = jnp.where(in_head, q, jnp.zeros_like(q))
            s = lax.dot_general(qm, k, (((1,), (1,)), ((), ())), preferred_element_type=F32)
            if has_prev:
                s = s + bias_ref[g, hh]
            else:
                s = s + bias_ref[g, hh, :, BLOCK:]
            m = jnp.max(s, axis=-1, keepdims=True)
            p = jnp.exp(s - m)
            den = jnp.sum(p, axis=-1, keepdims=True)
            pv = jnp.dot(p.astype(BF16), v, preferred_element_type=F32)
            o_h = pv * (1.0 / den)
            l_h = jnp.broadcast_to(m + jnp.log(den), (BLOCK, LANES))
            o = o_h if o is None else jnp.where(in_head, o_h, o)
            lse = l_h if lse is None else jnp.where(in_head, l_h, lse)
        if dilation == 1:
            dst = pl.ds(out_start, BLOCK)
        else:
            dst = pl.ds(out_start, BLOCK, stride=dilation)
        ogs[g][dst, :] = o
        lgs[g][dst, :] = lse

    for g, (window, dilation) in enumerate(ATTN_PATTERNS):
        n = seq // dilation
        nb = n // BLOCK

        def sequence(r, g=g, dilation=dilation, n=n, nb=nb):
            base = r * n
            block(g, pl.multiple_of(base, BLOCK), False, r)
            if nb > 1:
                def body(i, _):
                    row0 = pl.multiple_of(base + i * BLOCK, BLOCK)
                    block(g, row0, True, r + i * BLOCK * dilation)
                    return 0
                lax.fori_loop(1, nb, body, 0)

        if dilation <= 4:
            for r in range(dilation):
                sequence(r)
        else:
            def seq_body(r, _):
                sequence(r)
                return 0
            lax.fori_loop(0, dilation, seq_body, 0)

    chunk = 2 * BLOCK
    for c in range(0, seq, chunk):
        rows = pl.ds(c, chunk)
        ls = [lg[rows, :] for lg in lgs]
        mx = jnp.maximum(jnp.maximum(ls[0], ls[1]), ls[2])
        ws = [jnp.exp(l - mx) for l in ls]
        tot = ws[0] + ws[1] + ws[2]
        acc = ws[0] * og0[rows, :] + ws[1] * og1[rows, :] + ws[2] * og2[rows, :]
        o_ref[rows, :] = (acc * (1.0 / tot)).astype(BF16)


def _attention(qkv, bias):
    b, seq, width = qkv[0].shape
    col_blocks = width // LANES
    heads = LANES // HEAD_DIM
    tile = pl.BlockSpec((None, seq, LANES), lambda i, j: (i, 0, j))
    bias_spec = pl.BlockSpec((N_PATTERNS, heads, BLOCK, 2 * BLOCK), lambda i, j: (0, j, 0, 0))
    return pl.pallas_call(
        _attn_kernel,
        grid=(b, col_blocks),
        in_specs=[tile] * 9 + [bias_spec],
        out_specs=tile,
        out_shape=jax.ShapeDtypeStruct((b, seq, width), BF16),
        scratch_shapes=[pltpu.VMEM((seq, LANES), F32)] * 6,
        compiler_params=pltpu.CompilerParams(
            dimension_semantics=("arbitrary", "arbitrary"), vmem_limit_bytes=VMEM_LIMIT_BYTES),
        name="attn",
    )(*qkv, bias)


def _mix_kernel(x_ref, ys_ref, oa_ref, ga_ref, gb_ref, wv_ref, wg_ref, wup_ref, wout_ref, o_ref):
    ys = ys_ref[...]
    val = jnp.dot(ys, wv_ref[...], preferred_element_type=F32)
    gate = jnp.dot(ys, wg_ref[...], preferred_element_type=F32)
    y_a = val * jax.nn.sigmoid(gate)
    y_b = jnp.dot(oa_ref[...], wup_ref[...], preferred_element_type=F32)
    merged = ga_ref[...].astype(F32) * y_a + gb_ref[...].astype(F32) * y_b
    o_ref[...] = x_ref[...] + jnp.dot(merged.astype(BF16), wout_ref[...], preferred_element_type=F32)


def _mix(x2, ys, oa, ga, gb, wv, wg, wup, wout):
    t, d = x2.shape
    row = lambda width: pl.BlockSpec((MIX_ROWS, width), lambda i: (i, 0))
    return pl.pallas_call(
        _mix_kernel,
        grid=(t // MIX_ROWS,),
        in_specs=[row(d), row(ys.shape[1]), row(oa.shape[1]), row(d), row(d),
                  _resident(wv.shape), _resident(wg.shape), _resident(wup.shape), _resident(wout.shape)],
        out_specs=row(d),
        out_shape=jax.ShapeDtypeStruct((t, d), F32),
        compiler_params=pltpu.CompilerParams(
            dimension_semantics=("arbitrary",), vmem_limit_bytes=VMEM_LIMIT_BYTES),
        name="mix",
    )(x2, ys, oa, ga, gb, wv, wg, wup, wout)


def _ffn_kernel(x_ref, g_ref, wg_ref, wu_ref, cw_ref, cb_ref, wd_ref, o_ref, carry_ref, gs_ref, h_ref):
    rows = x_ref.shape[0]
    halo = SUBLANES

    @pl.when(pl.program_id(1) == 0)
    def _():
        carry_ref[...] = jnp.zeros_like(carry_ref)

    x = x_ref[...]
    ms = jnp.mean(x * x, axis=-1, keepdims=True)
    hf = (x * lax.rsqrt(ms + EPS) * g_ref[...]).astype(BF16)
    d_ff = wg_ref.shape[1]
    for c in range(0, d_ff, FFN_COLS):
        cols = slice(c, c + FFN_COLS)
        gate = jnp.dot(hf, wg_ref[:, cols], preferred_element_type=F32)
        up = jnp.dot(hf, wu_ref[:, cols], preferred_element_type=F32)
        gs_ref[0:halo, :] = carry_ref[:, cols]
        gs_ref[halo:halo + rows, :] = gate
        carry_ref[:, cols] = gate[rows - halo:rows, :]
        pre = cb_ref[:, cols] + cw_ref[CONV_WIDTH - 1:CONV_WIDTH, cols] * gate
        for tap in range(1, CONV_WIDTH):
            shifted = gs_ref[halo - tap:halo - tap + rows, :]
            pre = pre + cw_ref[CONV_WIDTH - 1 - tap:CONV_WIDTH - tap, cols] * shifted
        h_ref[:, cols] = (_gelu(pre) * up).astype(BF16)
    o_ref[...] = x + jnp.dot(h_ref[...], wd_ref[...], preferred_element_type=F32)


def _ffn(x3, norm_g, wg, wu, conv_w, conv_b, wd):
    b, seq, d = x3.shape
    d_ff = wg.shape[1]
    tile = pl.BlockSpec((None, FFN_ROWS, d), lambda i, j: (i, j, 0))
    return pl.pallas_call(
        _ffn_kernel,
        grid=(b, seq // FFN_ROWS),
        in_specs=[tile, _resident((1, d)), _resident(wg.shape), _resident(wu.shape),
                  _resident(conv_w.shape), _resident((1, d_ff)), _resident(wd.shape)],
        out_specs=tile,
        out_shape=jax.ShapeDtypeStruct((b, seq, d), F32),
        scratch_shapes=[pltpu.VMEM((SUBLANES, d_ff), F32),
                        pltpu.VMEM((FFN_ROWS + SUBLANES, FFN_COLS), F32),
                        pltpu.VMEM((FFN_ROWS, d_ff), BF16)],
        compiler_params=pltpu.CompilerParams(
            dimension_semantics=("arbitrary", "arbitrary"), vmem_limit_bytes=VMEM_LIMIT_BYTES),
        name="ffn",
    )(x3, norm_g, wg, wu, conv_w, conv_b, wd)


def _layer(x, norm_mix_g, w_in, lam_re, lam_im, log_dt, b_re, b_im, c_re, c_im, ssm_d,
           glu_w_val, glu_w_gate, q_norm_g, k_norm_g, w_attn_up, w_out, norm_ffn_g,
           ffn_w_gate, ffn_w_up, ffn_conv_w, ffn_conv_b, ffn_w_down):
    b, seq, d = x.shape
    t = b * seq
    x2 = x.reshape(t, d)
    scale = HEAD_DIM ** -0.5
    q_gain = jnp.tile(q_norm_g.astype(F32) * scale, HEADS_PER_PATTERN)[None, :]
    k_gain = jnp.tile(k_norm_g.astype(F32), HEADS_PER_PATTERN)[None, :]

    outs = _in_proj(x2, norm_mix_g.astype(F32)[None, :], w_in.astype(BF16), q_gain, k_gain)
    u, qkv, gate_a, gate_b = outs[0], outs[1:10], outs[10], outs[11]

    ssm_width = u.shape[1]
    u_tm = u.reshape(b, seq, ssm_width).transpose(1, 0, 2).reshape(t, ssm_width)
    bpad, ar_v, ai_v, cpad = _ssm_params(lam_re, lam_im, log_dt, b_re, b_im, c_re, c_im)
    ys_tm = _ssm(u_tm, bpad, ar_v, ai_v, cpad, ssm_d.astype(F32)[None, :], b)
    ys = ys_tm.reshape(seq, b, ssm_width).transpose(1, 0, 2).reshape(t, ssm_width)

    def strided(a, dilation):
        a = a.reshape(b, seq, PATTERN_WIDTH)
        if dilation == 1:
            return a
        a = a.reshape(b, seq // dilation, dilation, PATTERN_WIDTH).transpose(0, 2, 1, 3)
        return a.reshape(b, seq, PATTERN_WIDTH)

    attn_in = []
    for g, (_, dilation) in enumerate(ATTN_PATTERNS):
        for kind in range(3):
            attn_in.append(strided(qkv[kind * N_PATTERNS + g], dilation))
    o_attn = _attention(attn_in, jnp.asarray(_attn_bias())).reshape(t, PATTERN_WIDTH)

    x1 = _mix(x2, ys, o_attn, gate_a, gate_b, glu_w_val.astype(BF16), glu_w_gate.astype(BF16),
              w_attn_up.astype(BF16), w_out.astype(BF16))

    out = _ffn(x1.reshape(b, seq, d), norm_ffn_g.astype(F32)[None, :], ffn_w_gate.astype(BF16),
               ffn_w_up.astype(BF16), ffn_conv_w.astype(F32), ffn_conv_b.astype(F32)[None, :],
               ffn_w_down.astype(BF16))
    return out


def kernel(x, norm_mix_g, w_in, ssm_lambda_re, ssm_lambda_im, ssm_log_dt, ssm_b_re, ssm_b_im, ssm_c_re, ssm_c_im, ssm_d, glu_w_val, glu_w_gate, q_norm_g, k_norm_g, w_attn_up, w_out, norm_ffn_g, ffn_w_gate, ffn_w_up, ffn_conv_w, ffn_conv_b, ffn_w_down):
    depth = w_in.shape[0]
    for i in range(depth):
        x = _layer(x, norm_mix_g[i], w_in[i], ssm_lambda_re[i], ssm_lambda_im[i], ssm_log_dt[i],
                   ssm_b_re[i], ssm_b_im[i], ssm_c_re[i], ssm_c_im[i], ssm_d[i], glu_w_val[i],
                   glu_w_gate[i], q_norm_g[i], k_norm_g[i], w_attn_up[i], w_out[i], norm_ffn_g[i],
                   ffn_w_gate[i], ffn_w_up[i], ffn_conv_w[i], ffn_conv_b[i], ffn_w_down[i]).astype(x.dtype)
    return x
```

```python
import functools
import math

import numpy as np
import jax
import jax.numpy as jnp
from jax import lax
from jax.experimental import pallas as pl
from jax.experimental.pallas import tpu as pltpu

F32 = jnp.float32
BF16 = jnp.bfloat16

SSM_GROUP = 16
SSM_STATE = 64
HEAD_DIM = 64
ATTN_PATTERNS = ((128, 1), (512, 4), (2048, 16))
N_PATTERNS = len(ATTN_PATTERNS)
HEADS_PER_PATTERN = 8
N_ATTN_HEADS = N_PATTERNS * HEADS_PER_PATTERN
PATTERN_WIDTH = HEADS_PER_PATTERN * HEAD_DIM
BLOCK = 128
CONV_WIDTH = 3
EPS = 1e-6
LOG2E = math.log2(math.e)

LANES = 128
SUBLANES = 8
VMEM_LIMIT_BYTES = 56 * 1024 * 1024

IN_ROWS = 512
MIX_ROWS = 512
FFN_ROWS = 512
FFN_COLS = 256
SSM_STEPS = 32
ATTN_LAG = 2
GROUPS_PER_TILE = LANES // SSM_GROUP
PAIRS_PER_TILE = GROUPS_PER_TILE // 2
PAIR_STATE = 2 * SSM_STATE
PAIR_COLS = 2 * PAIR_STATE


def _resident(shape):
    nd = len(shape)
    return pl.BlockSpec(shape, lambda *_: (0,) * nd, pipeline_mode=pl.Buffered(1))


def _gelu(z):
    return 0.5 * z * (1.0 + lax.erf(z * (1.0 / math.sqrt(2.0))))


def _in_proj_kernel(x_ref, g_ref, w_ref, qg_ref, kg_ref, ones_ref,
                    u_ref, q0, q1, q2, k0, k1, k2, v0, v1, v2, ga_ref, gb_ref):
    x = x_ref[...]
    ms = jnp.mean(x * x, axis=-1, keepdims=True)
    h = (x * lax.rsqrt(ms + EPS) * g_ref[...]).astype(BF16)

    def proj(col0, width):
        return jnp.dot(h, w_ref[:, col0:col0 + width], preferred_element_type=F32)

    def head_norm(z, gain):
        sq = (z * z).astype(BF16)
        half = ones_ref.shape[0]
        parts = [jnp.dot(sq[:, c:c + half], ones_ref[...], preferred_element_type=F32)
                 for c in range(0, z.shape[1], half)]
        ssq = jnp.concatenate(parts, axis=1)
        return z * lax.rsqrt(ssq * (1.0 / HEAD_DIM) + EPS) * gain

    w = PATTERN_WIDTH
    u_ref[...] = proj(0, w)
    col = w
    for ref in (q0, q1, q2):
        ref[...] = head_norm(proj(col, w), qg_ref[...]).astype(BF16)
        col += w
    for ref in (k0, k1, k2):
        ref[...] = head_norm(proj(col, w), kg_ref[...]).astype(BF16)
        col += w
    for ref in (v0, v1, v2):
        ref[...] = proj(col, w).astype(BF16)
        col += w
    for ref in (ga_ref, gb_ref):
        for c in range(0, ref.shape[1], w):
            ref[:, c:c + w] = jax.nn.sigmoid(proj(col, w)).astype(BF16)
            col += w


def _in_proj(x2, norm_g, w_in, q_gain, k_gain):
    t, d = x2.shape
    n_in = w_in.shape[1]
    w = PATTERN_WIDTH
    d_gate = (n_in - w - 3 * N_PATTERNS * w) // 2
    ones_bd = jnp.asarray(np.kron(np.eye(256 // HEAD_DIM), np.ones((HEAD_DIM, HEAD_DIM))), BF16)
    row = lambda width: pl.BlockSpec((IN_ROWS, width), lambda i: (i, 0))
    out_shape = ([jax.ShapeDtypeStruct((t, w), F32)]
                 + [jax.ShapeDtypeStruct((t, w), BF16)] * 9
                 + [jax.ShapeDtypeStruct((t, d_gate), BF16)] * 2)
    return pl.pallas_call(
        _in_proj_kernel,
        grid=(t // IN_ROWS,),
        in_specs=[row(d), _resident((1, d)), _resident(w_in.shape),
                  _resident((1, w)), _resident((1, w)), _resident(ones_bd.shape)],
        out_specs=[row(w)] * 10 + [row(d_gate)] * 2,
        out_shape=out_shape,
        compiler_params=pltpu.CompilerParams(
            dimension_semantics=("arbitrary",), vmem_limit_bytes=VMEM_LIMIT_BYTES),
        name="in_proj",
    )(x2, norm_g, w_in, q_gain, k_gain, ones_bd)


def _ssm_kernel(u_ref, bp_ref, ar_ref, ai_ref, cp_ref, d_ref, y_ref, state_ref, bu_ref, xs_ref, *, batch):
    @pl.when(pl.program_id(0) == 0)
    def _():
        state_ref[...] = jnp.zeros_like(state_ref)

    n_tiles = u_ref.shape[1] // LANES
    for q in range(n_tiles):
        u_tile = u_ref[:, q * LANES:(q + 1) * LANES]
        ub = u_tile.astype(BF16)
        for gl in range(PAIRS_PER_TILE):
            gp = q * PAIRS_PER_TILE + gl
            bu_ref[:, gl * PAIR_COLS:(gl + 1) * PAIR_COLS] = jnp.dot(
                ub, bp_ref[gp], preferred_element_type=F32)

        ars = [ar_ref[q * PAIRS_PER_TILE + gl] for gl in range(PAIRS_PER_TILE)]
        ais = [ai_ref[q * PAIRS_PER_TILE + gl] for gl in range(PAIRS_PER_TILE)]

        def step(t, carry):
            rows = pl.ds(pl.multiple_of(t * batch, batch), batch)
            new = []
            for gl in range(PAIRS_PER_TILE):
                xr, xi = carry[gl]
                c0 = gl * PAIR_COLS
                br = bu_ref[rows, c0:c0 + PAIR_STATE]
                bi = bu_ref[rows, c0 + PAIR_STATE:c0 + PAIR_COLS]
                nr = ars[gl] * xr - ais[gl] * xi + br
                ni = ars[gl] * xi + ais[gl] * xr + bi
                xs_ref[rows, c0:c0 + PAIR_STATE] = nr
                xs_ref[rows, c0 + PAIR_STATE:c0 + PAIR_COLS] = ni
                new.append((nr, ni))
            return tuple(new)

        init = tuple((state_ref[q * PAIRS_PER_TILE + gl, 0], state_ref[q * PAIRS_PER_TILE + gl, 1])
                     for gl in range(PAIRS_PER_TILE))
        final = lax.fori_loop(0, SSM_STEPS, step, init, unroll=4)
        for gl in range(PAIRS_PER_TILE):
            state_ref[q * PAIRS_PER_TILE + gl, 0] = final[gl][0]
            state_ref[q * PAIRS_PER_TILE + gl, 1] = final[gl][1]

        y = jnp.dot(xs_ref[...].astype(BF16), cp_ref[q], preferred_element_type=F32)
        y = y + d_ref[:, q * LANES:(q + 1) * LANES] * u_tile
        y_ref[:, q * LANES:(q + 1) * LANES] = _gelu(y).astype(BF16)


def _ssm(u_tm, bpad, ar_v, ai_v, cpad, d_skip, batch):
    rows_total, width = u_tm.shape
    rows = SSM_STEPS * batch
    n_pairs = bpad.shape[0]
    kernel = functools.partial(_ssm_kernel, batch=batch)
    return pl.pallas_call(
        kernel,
        grid=(rows_total // rows,),
        in_specs=[pl.BlockSpec((rows, width), lambda i: (i, 0)),
                  _resident(bpad.shape), _resident(ar_v.shape), _resident(ai_v.shape),
                  _resident(cpad.shape), _resident(d_skip.shape)],
        out_specs=pl.BlockSpec((rows, width), lambda i: (i, 0)),
        out_shape=jax.ShapeDtypeStruct((rows_total, width), BF16),
        scratch_shapes=[pltpu.VMEM((n_pairs, 2, batch, PAIR_STATE), F32),
                        pltpu.VMEM((rows, PAIRS_PER_TILE * PAIR_COLS), F32),
                        pltpu.VMEM((rows, PAIRS_PER_TILE * PAIR_COLS), F32)],
        compiler_params=pltpu.CompilerParams(
            dimension_semantics=("arbitrary",), vmem_limit_bytes=VMEM_LIMIT_BYTES),
        name="ssm",
    )(u_tm, bpad, ar_v, ai_v, cpad, d_skip)


def _ssm_params(lam_re, lam_im, log_dt, b_re, b_im, c_re, c_im):
    g = lam_re.shape[0]
    dt = jnp.exp(log_dt.astype(F32))[:, None]
    lr = lam_re.astype(F32)
    li = lam_im.astype(F32)
    mag = jnp.exp(lr * dt)
    ar = mag * jnp.cos(li * dt)
    ai = mag * jnp.sin(li * dt)
    den = lr * lr + li * li
    cr = ((ar - 1.0) * lr + ai * li) / den
    ci = (ai * lr - (ar - 1.0) * li) / den
    br = b_re.astype(F32)
    bi = b_im.astype(F32)
    bbar_re = cr[..., None] * br - ci[..., None] * bi
    bbar_im = cr[..., None] * bi + ci[..., None] * br

    n_pairs = g // 2
    n_tiles = n_pairs // PAIRS_PER_TILE
    eye_pt = jnp.eye(PAIRS_PER_TILE, dtype=F32)
    eye_2 = jnp.eye(2, dtype=F32)
    bb = jnp.stack([bbar_re, bbar_im], axis=0).reshape(2, n_tiles, PAIRS_PER_TILE, 2, SSM_STATE, SSM_GROUP)
    bpad = jnp.einsum('aqlgpc,lm,gh->qlmgcahp', bb, eye_pt, eye_2)
    bpad = bpad.reshape(n_pairs, LANES, PAIR_COLS)
    cc = jnp.stack([c_re.astype(F32), -c_im.astype(F32)], axis=0)
    cc = cc.reshape(2, n_tiles, PAIRS_PER_TILE, 2, SSM_GROUP, SSM_STATE)
    cpad = jnp.einsum('aqlgcp,lm,gh->qlagpmhc', cc, eye_pt, eye_2)
    cpad = cpad.reshape(n_tiles, PAIRS_PER_TILE * PAIR_COLS, LANES)
    ar_v = ar.reshape(n_pairs, 1, PAIR_STATE)
    ai_v = ai.reshape(n_pairs, 1, PAIR_STATE)
    return bpad.astype(BF16), ar_v, ai_v, cpad.astype(BF16)


def _alibi_slopes():
    i = np.arange(1, N_ATTN_HEADS + 1, dtype=np.float32)
    s = np.exp2(-8.0 * i / N_ATTN_HEADS).astype(np.float32)
    return s.reshape(HEADS_PER_PATTERN, N_PATTERNS).T


def _attn_bias():
    slopes = _alibi_slopes()
    qi = np.arange(BLOCK)[:, None]
    ki = np.arange(2 * BLOCK)[None, :]
    delta = qi + BLOCK - ki
    heads = LANES // HEAD_DIM
    out = np.empty((N_PATTERNS, HEADS_PER_PATTERN // heads, heads * BLOCK, 2 * BLOCK), np.float32)
    for g, (window, dilation) in enumerate(ATTN_PATTERNS):
        reach = window // dilation
        valid = (delta >= 0) & (delta <= reach)
        dist = (delta * dilation).astype(np.float32)
        for h in range(HEADS_PER_PATTERN):
            rows = slice((h % heads) * BLOCK, (h % heads + 1) * BLOCK)
            out[g, h // heads, rows] = np.where(valid, -slopes[g, h] * dist * LOG2E, -np.inf)
    return out


def _attn_kernel(q0, k0, v0, q1, k1, v1, q2, k2, v2, bias_ref, o_ref,
                 og0, og1, og2, dg0, dg1, dg2, mg0, mg1, mg2):
    qkv = ((q0, k0, v0), (q1, k1, v1), (q2, k2, v2))
    ogs = (og0, og1, og2)
    dgs = (dg0, dg1, dg2)
    mgs = (mg0, mg1, mg2)
    seq = q0.shape[0]
    n_blocks = seq // BLOCK
    lane = lax.broadcasted_iota(jnp.int32, (BLOCK, LANES), 1)
    head0 = lane < HEAD_DIM

    blocks = [(g, bi) for g in range(N_PATTERNS) for bi in range(n_blocks)]

    def key_rows(g, bi):
        nb = n_blocks // ATTN_PATTERNS[g][1]
        first = bi % nb == 0
        return slice((bi if first else bi - 1) * BLOCK, (bi + 1) * BLOCK), first

    def logits(g, bi):
        q_ref, k_ref, _ = qkv[g]
        keys, first = key_rows(g, bi)
        q = q_ref[bi * BLOCK:(bi + 1) * BLOCK, :]
        zero = jnp.zeros_like(q)
        qs = jnp.concatenate([jnp.where(head0, q, zero), jnp.where(head0, zero, q)], axis=0)
        s = lax.dot_general(qs, k_ref[keys, :], (((1,), (1,)), ((), ())), preferred_element_type=F32)
        return s + (bias_ref[g, :, BLOCK:] if first else bias_ref[g])

    def softmax_numerator(s):
        m = jnp.max(s, axis=-1, keepdims=True)
        return jnp.exp2(s - m).astype(BF16), m

    def weighted_values(g, bi, p, m):
        _, _, v_ref = qkv[g]
        dilation = ATTN_PATTERNS[g][1]
        nb = n_blocks // dilation
        keys, _ = key_rows(g, bi)
        v = v_ref[keys, :]
        pv = jnp.dot(p, jnp.concatenate([v, jnp.ones_like(v)], axis=1), preferred_element_type=F32)
        out_start = bi // nb + (bi % nb) * BLOCK * dilation
        dst = pl.ds(out_start, BLOCK) if dilation == 1 else pl.ds(out_start, BLOCK, stride=dilation)
        ogs[g][dst, :] = jnp.where(head0, pv[:BLOCK, :LANES], pv[BLOCK:, :LANES])
        dgs[g][dst, :] = jnp.where(head0, pv[:BLOCK, LANES:], pv[BLOCK:, LANES:])
        mgs[g][dst, :] = jnp.where(head0, jnp.broadcast_to(m[:BLOCK], (BLOCK, LANES)),
                                   jnp.broadcast_to(m[BLOCK:], (BLOCK, LANES)))

    scores, numer = {}, {}
    for step in range(len(blocks) + 2 * ATTN_LAG):
        if step < len(blocks):
            scores[step] = logits(*blocks[step])
        if 0 <= step - ATTN_LAG < len(blocks):
            numer[step - ATTN_LAG] = softmax_numerator(scores.pop(step - ATTN_LAG))
        if 0 <= step - 2 * ATTN_LAG < len(blocks):
            weighted_values(*blocks[step - 2 * ATTN_LAG], *numer.pop(step - 2 * ATTN_LAG))

    chunk = 2 * BLOCK
    for c in range(0, seq, chunk):
        rows = pl.ds(c, chunk)
        ms = [mg[rows, :] for mg in mgs]
        mx = jnp.maximum(jnp.maximum(ms[0], ms[1]), ms[2])
        es = [jnp.exp2(m - mx) for m in ms]
        tot = es[0] * dgs[0][rows, :] + es[1] * dgs[1][rows, :] + es[2] * dgs[2][rows, :]
        acc = es[0] * ogs[0][rows, :] + es[1] * ogs[1][rows, :] + es[2] * ogs[2][rows, :]
        o_ref[rows, :] = (acc * (1.0 / tot)).astype(BF16)


def _attention(qkv, bias):
    b, seq, width = qkv[0].shape
    col_blocks = width // LANES
    tile = pl.BlockSpec((None, seq, LANES), lambda j, i: (i, 0, j))
    bias_spec = pl.BlockSpec((N_PATTERNS, None) + bias.shape[2:], lambda j, i: (0, j, 0, 0))
    return pl.pallas_call(
        _attn_kernel,
        grid=(col_blocks, b),
        in_specs=[tile] * 9 + [bias_spec],
        out_specs=tile,
        out_shape=jax.ShapeDtypeStruct((b, seq, width), BF16),
        scratch_shapes=[pltpu.VMEM((seq, LANES), F32)] * 9,
        compiler_params=pltpu.CompilerParams(
            dimension_semantics=("arbitrary", "arbitrary"), vmem_limit_bytes=VMEM_LIMIT_BYTES),
        name="attn",
    )(*qkv, bias)


def _mix_kernel(x_ref, ys_ref, oa_ref, ga_ref, gb_ref, wv_ref, wg_ref, wup_ref, wout_ref, o_ref):
    ys = ys_ref[...]
    val = jnp.dot(ys, wv_ref[...], preferred_element_type=F32)
    gate = jnp.dot(ys, wg_ref[...], preferred_element_type=F32)
    y_a = val * jax.nn.sigmoid(gate)
    y_b = jnp.dot(oa_ref[...], wup_ref[...], preferred_element_type=F32)
    merged = ga_ref[...].astype(F32) * y_a + gb_ref[...].astype(F32) * y_b
    o_ref[...] = x_ref[...] + jnp.dot(merged.astype(BF16), wout_ref[...], preferred_element_type=F32)


def _mix(x2, ys, oa, ga, gb, wv, wg, wup, wout):
    t, d = x2.shape
    row = lambda width: pl.BlockSpec((MIX_ROWS, width), lambda i: (i, 0))
    return pl.pallas_call(
        _mix_kernel,
        grid=(t // MIX_ROWS,),
        in_specs=[row(d), row(ys.shape[1]), row(oa.shape[1]), row(d), row(d),
                  _resident(wv.shape), _resident(wg.shape), _resident(wup.shape), _resident(wout.shape)],
        out_specs=row(d),
        out_shape=jax.ShapeDtypeStruct((t, d), F32),
        compiler_params=pltpu.CompilerParams(
            dimension_semantics=("arbitrary",), vmem_limit_bytes=VMEM_LIMIT_BYTES),
        name="mix",
    )(x2, ys, oa, ga, gb, wv, wg, wup, wout)


def _ffn_kernel(x_ref, g_ref, wg_ref, wu_ref, cw_ref, cb_ref, wd_ref, o_ref, carry_ref, gs_ref, h_ref):
    rows = x_ref.shape[0]
    halo = SUBLANES

    @pl.when(pl.program_id(1) == 0)
    def _():
        carry_ref[...] = jnp.zeros_like(carry_ref)

    x = x_ref[...]
    ms = jnp.mean(x * x, axis=-1, keepdims=True)
    hf = (x * lax.rsqrt(ms + EPS) * g_ref[...]).astype(BF16)
    d_ff = wg_ref.shape[1]
    for c in range(0, d_ff, FFN_COLS):
        cols = slice(c, c + FFN_COLS)
        gate = jnp.dot(hf, wg_ref[:, cols], preferred_element_type=F32)
        up = jnp.dot(hf, wu_ref[:, cols], preferred_element_type=F32)
        gs_ref[0:halo, :] = carry_ref[:, cols]
        gs_ref[halo:halo + rows, :] = gate
        carry_ref[:, cols] = gate[rows - halo:rows, :]
        pre = cb_ref[:, cols] + cw_ref[CONV_WIDTH - 1:CONV_WIDTH, cols] * gate
        for tap in range(1, CONV_WIDTH):
            shifted = gs_ref[halo - tap:halo - tap + rows, :]
            pre = pre + cw_ref[CONV_WIDTH - 1 - tap:CONV_WIDTH - tap, cols] * shifted
        h_ref[:, cols] = (_gelu(pre) * up).astype(BF16)
    o_ref[...] = x + jnp.dot(h_ref[...], wd_ref[...], preferred_element_type=F32)


def _ffn(x3, norm_g, wg, wu, conv_w, conv_b, wd):
    b, seq, d = x3.shape
    d_ff = wg.shape[1]
    tile = pl.BlockSpec((None, FFN_ROWS, d), lambda i, j: (i, j, 0))
    return pl.pallas_call(
        _ffn_kernel,
        grid=(b, seq // FFN_ROWS),
        in_specs=[tile, _resident((1, d)), _resident(wg.shape), _resident(wu.shape),
                  _resident(conv_w.shape), _resident((1, d_ff)), _resident(wd.shape)],
        out_specs=tile,
        out_shape=jax.ShapeDtypeStruct((b, seq, d), F32),
        scratch_shapes=[pltpu.VMEM((SUBLANES, d_ff), F32),
                        pltpu.VMEM((FFN_ROWS + SUBLANES, FFN_COLS), F32),
                        pltpu.VMEM((FFN_ROWS, d_ff), BF16)],
        compiler_params=pltpu.CompilerParams(
            dimension_semantics=("arbitrary", "arbitrary"), vmem_limit_bytes=VMEM_LIMIT_BYTES),
        name="ffn",
    )(x3, norm_g, wg, wu, conv_w, conv_b, wd)


def _layer(x, norm_mix_g, w_in, lam_re, lam_im, log_dt, b_re, b_im, c_re, c_im, ssm_d,
           glu_w_val, glu_w_gate, q_norm_g, k_norm_g, w_attn_up, w_out, norm_ffn_g,
           ffn_w_gate, ffn_w_up, ffn_conv_w, ffn_conv_b, ffn_w_down):
    b, seq, d = x.shape
    t = b * seq
    x2 = x.reshape(t, d)
    scale = HEAD_DIM ** -0.5 * LOG2E
    q_gain =jnp.tile(q_norm_g.astype(F32) * scale, HEADS_PER_PATTERN)[None, :]
    k_gain = jnp.tile(k_norm_g.astype(F32), HEADS_PER_PATTERN)[None, :]

    outs = _in_proj(x2, norm_mix_g.astype(F32)[None, :], w_in.astype(BF16), q_gain, k_gain)
    u, qkv, gate_a, gate_b = outs[0], outs[1:10], outs[10], outs[11]

    ssm_width = u.shape[1]
    u_tm = u.reshape(b, seq, ssm_width).transpose(1, 0, 2).reshape(t, ssm_width)
    bpad, ar_v, ai_v, cpad = _ssm_params(lam_re, lam_im, log_dt, b_re, b_im, c_re, c_im)
    ys_tm = _ssm(u_tm, bpad, ar_v, ai_v, cpad, ssm_d.astype(F32)[None, :], b)
    ys = ys_tm.reshape(seq, b, ssm_width).transpose(1, 0, 2).reshape(t, ssm_width)

    def strided(a, dilation):
        a = a.reshape(b, seq, PATTERN_WIDTH)
        if dilation == 1:
            return a
        a = a.reshape(b, seq // dilation, dilation, PATTERN_WIDTH).transpose(0, 2, 1, 3)
        return a.reshape(b, seq, PATTERN_WIDTH)

    attn_in = []
    for g, (_, dilation) in enumerate(ATTN_PATTERNS):
        for kind in range(3):
            attn_in.append(strided(qkv[kind * N_PATTERNS + g], dilation))
    o_attn = _attention(attn_in, jnp.asarray(_attn_bias())).reshape(t, PATTERN_WIDTH)

    x1 = _mix(x2, ys, o_attn, gate_a, gate_b, glu_w_val.astype(BF16), glu_w_gate.astype(BF16),
              w_attn_up.astype(BF16), w_out.astype(BF16))

    out = _ffn(x1.reshape(b, seq, d), norm_ffn_g.astype(F32)[None, :], ffn_w_gate.astype(BF16),
               ffn_w_up.astype(BF16), ffn_conv_w.astype(F32), ffn_conv_b.astype(F32)[None, :],
               ffn_w_down.astype(BF16))
    return out


def kernel(x, norm_mix_g, w_in, ssm_lambda_re, ssm_lambda_im, ssm_log_dt, ssm_b_re, ssm_b_im, ssm_c_re, ssm_c_im, ssm_d, glu_w_val, glu_w_gate, q_norm_g, k_norm_g, w_attn_up, w_out, norm_ffn_g, ffn_w_gate, ffn_w_up, ffn_conv_w, ffn_conv_b, ffn_w_down):
    depth = w_in.shape[0]
    for i in range(depth):
        x = _layer(x, norm_mix_g[i], w_in[i], ssm_lambda_re[i], ssm_lambda_im[i], ssm_log_dt[i],
                   ssm_b_re[i], ssm_b_im[i], ssm_c_re[i], ssm_c_im[i], ssm_d[i], glu_w_val[i],
                   glu_w_gate[i], q_norm_g[i], k_norm_g[i], w_attn_up[i], w_out[i], norm_ffn_g[i],
                   ffn_w_gate[i], ffn_w_up[i], ffn_conv_w[i], ffn_conv_b[i], ffn_w_down[i]).astype(x.dtype)
    return x
```

```python
import functools
import math

import numpy as np
import jax
import jax.numpy as jnp
from jax import lax
from jax.experimental import pallas as pl
from jax.experimental.pallas import tpu as pltpu

F32 = jnp.float32
BF16 = jnp.bfloat16

SSM_GROUP = 16
SSM_STATE = 64
HEAD_DIM = 64
ATTN_PATTERNS = ((128, 1), (512, 4), (2048, 16))
N_PATTERNS = len(ATTN_PATTERNS)
HEADS_PER_PATTERN = 8
N_ATTN_HEADS = N_PATTERNS * HEADS_PER_PATTERN
PATTERN_WIDTH = HEADS_PER_PATTERN * HEAD_DIM
BLOCK = 128
CONV_WIDTH = 3
EPS = 1e-6
LOG2E = math.log2(math.e)

LANES = 128
SUBLANES = 8
VMEM_LIMIT_BYTES = 56 * 1024 * 1024

IN_ROWS = 512
MIX_ROWS = 512
FFN_ROWS = 512
FFN_COLS = 256
SSM_STEPS = 32
ATTN_LAG = 2
GROUPS_PER_TILE = LANES // SSM_GROUP
PAIRS_PER_TILE = GROUPS_PER_TILE // 2
PAIR_STATE = 2 * SSM_STATE
PAIR_COLS = 2 * PAIR_STATE


def _resident(shape):
    nd = len(shape)
    return pl.BlockSpec(shape, lambda *_: (0,) * nd, pipeline_mode=pl.Buffered(1))


def _gelu(z):
    return 0.5 * z * (1.0 + lax.erf(z * (1.0 / math.sqrt(2.0))))


def _in_proj_kernel(x_ref, g_ref, w_ref, qg_ref, kg_ref, ones_ref,
                    u_ref, q0, q1, q2, k0, k1, k2, v0, v1, v2, ga_ref, gb_ref, slab_ref, h_ref):
    rows, d = x_ref.shape
    n_slabs = d // LANES
    x = x_ref[...]
    ms = jnp.mean(x * x, axis=-1, keepdims=True)
    hn = x * lax.rsqrt(ms + EPS) * g_ref[...]
    h_ref[0] = hn.astype(BF16)
    for s in range(n_slabs):
        slab_ref[0, s] = hn[:, s * LANES:(s + 1) * LANES]

    prev = 1
    for g in range(1, N_PATTERNS):
        dilation = ATTN_PATTERNS[g][1]
        factor = dilation // prev
        n_prev, n = rows // prev, rows // dilation
        for s in range(n_slabs):
            for r_prev in range(prev):
                for a in range(factor):
                    piece = slab_ref[g - 1, s, pl.ds(r_prev * n_prev + a, n, stride=factor), :]
                    r = a * prev + r_prev
                    h_ref[g, r * n:(r + 1) * n, s * LANES:(s + 1) * LANES] = piece.astype(BF16)
                    if g + 1 < N_PATTERNS:
                        slab_ref[g, s, r * n:(r + 1) * n, :] = piece
        prev = dilation

    def proj(g, col0, width):
        return jnp.dot(h_ref[g], w_ref[:, col0:col0 + width], preferred_element_type=F32)

    def head_norm(z, gain):
        sq = (z * z).astype(BF16)
        half = ones_ref.shape[0]
        parts = [jnp.dot(sq[:, c:c + half], ones_ref[...], preferred_element_type=F32)
                 for c in range(0, z.shape[1], half)]
        ssq = jnp.concatenate(parts, axis=1)
        return z * lax.rsqrt(ssq * (1.0 / HEAD_DIM) + EPS) * gain

    def store(ref, z):
        if len(ref.shape) == 2:
            ref[...] = z.astype(BF16)
        else:
            n = ref.shape[1]
            for r in range(ref.shape[0]):
                ref[r] = z[r * n:(r + 1) * n].astype(BF16)

    w = PATTERN_WIDTH
    u_ref[...] = proj(0, 0, w)
    col = w
    for g, ref in enumerate((q0, q1, q2)):
        store(ref, head_norm(proj(g, col, w), qg_ref[...]))
        col += w
    for g, ref in enumerate((k0, k1, k2)):
        store(ref, head_norm(proj(g, col, w), kg_ref[...]))
        col += w
    for g, ref in enumerate((v0, v1, v2)):
        store(ref, proj(g, col, w))
        col += w
    for ref in (ga_ref, gb_ref):
        for c in range(0, ref.shape[1], w):
            ref[:, c:c + w] = jax.nn.sigmoid(proj(0, col, w)).astype(BF16)
            col += w


def _in_proj(x, norm_g, w_in, q_gain, k_gain):
    b, seq, d = x.shape
    n_in = w_in.shape[1]
    w = PATTERN_WIDTH
    d_gate = (n_in - w - 3 * N_PATTERNS * w) // 2
    ones_bd = jnp.asarray(np.kron(np.eye(256 // HEAD_DIM), np.ones((HEAD_DIM, HEAD_DIM))), BF16)
    row = lambda width: pl.BlockSpec((None, IN_ROWS, width), lambda i, j: (i, j, 0))

    def pattern_spec(dilation):
        if dilation == 1:
            return row(w), jax.ShapeDtypeStruct((b, seq, w), BF16)
        spec = pl.BlockSpec((None, dilation, IN_ROWS // dilation, w), lambda i, j: (i, 0, j, 0))
        return spec, jax.ShapeDtypeStruct((b, dilation, seq // dilation, w), BF16)

    qkv = [pattern_spec(dilation) for _ in range(3) for _, dilation in ATTN_PATTERNS]
    out_specs = [row(w)] + [s for s, _ in qkv] + [row(d_gate)] * 2
    out_shape = ([jax.ShapeDtypeStruct((b, seq, w), F32)] + [s for _, s in qkv]
                 + [jax.ShapeDtypeStruct((b, seq, d_gate), BF16)] * 2)
    return pl.pallas_call(
        _in_proj_kernel,
        grid=(b, seq // IN_ROWS),
        in_specs=[row(d), _resident((1, d)), _resident(w_in.shape),
                  _resident((1, w)), _resident((1, w)), _resident(ones_bd.shape)],
        out_specs=out_specs,
        out_shape=out_shape,
        scratch_shapes=[pltpu.VMEM((N_PATTERNS - 1, d // LANES, IN_ROWS, LANES), F32),
                        pltpu.VMEM((N_PATTERNS, IN_ROWS, d), BF16)],
        compiler_params=pltpu.CompilerParams(
            dimension_semantics=("arbitrary", "arbitrary"), vmem_limit_bytes=VMEM_LIMIT_BYTES),
        name="in_proj",
    )(x, norm_g, w_in, q_gain, k_gain, ones_bd)


def _ssm_kernel(u_ref, bp_ref, ar_ref, ai_ref, cp_ref, d_ref, y_ref, state_ref, bu_ref, xs_ref, *, batch):
    @pl.when(pl.program_id(0) == 0)
    def _():
        state_ref[...] = jnp.zeros_like(state_ref)

    n_tiles = u_ref.shape[1] // LANES
    for q in range(n_tiles):
        u_tile = u_ref[:, q * LANES:(q + 1) * LANES]
        ub = u_tile.astype(BF16)
        for gl in range(PAIRS_PER_TILE):
            gp = q * PAIRS_PER_TILE + gl
            bu_ref[q, :, gl * PAIR_COLS:(gl + 1) * PAIR_COLS] = jnp.dot(
                ub, bp_ref[gp], preferred_element_type=F32)

        for gl in range(PAIRS_PER_TILE):
            gp = q * PAIRS_PER_TILE + gl
            a_re, a_im = ar_ref[gp], ai_ref[gp]
            xr, xi = state_ref[gp, 0], state_ref[gp, 1]
            c0 = gl * PAIR_COLS
            for t in range(SSM_STEPS):
                rows = slice(t * batch, (t + 1) * batch)
                br = bu_ref[q, rows, c0:c0 + PAIR_STATE]
                bi = bu_ref[q, rows, c0 + PAIR_STATE:c0 + PAIR_COLS]
                xr, xi = a_re * xr - a_im * xi + br, a_re * xi + a_im * xr + bi
                xs_ref[q, rows, c0:c0 + PAIR_STATE] = xr
                xs_ref[q, rows, c0 + PAIR_STATE:c0 + PAIR_COLS] = xi
            state_ref[gp, 0] = xr
            state_ref[gp, 1] = xi

        y = jnp.dot(xs_ref[q].astype(BF16), cp_ref[q], preferred_element_type=F32)
        y = y + d_ref[:, q * LANES:(q + 1) * LANES] * u_tile
        y_ref[:, q * LANES:(q + 1) * LANES] = _gelu(y).astype(BF16)


def _ssm(u_tm, bpad, ar_v, ai_v, cpad, d_skip, batch):
    rows_total, width = u_tm.shape
    rows = SSM_STEPS * batch
    n_pairs = bpad.shape[0]
    kernel = functools.partial(_ssm_kernel, batch=batch)
    return pl.pallas_call(
        kernel,
        grid=(rows_total // rows,),
        in_specs=[pl.BlockSpec((rows, width), lambda i: (i, 0)),
                  _resident(bpad.shape), _resident(ar_v.shape), _resident(ai_v.shape),
                  _resident(cpad.shape), _resident(d_skip.shape)],
        out_specs=pl.BlockSpec((rows, width), lambda i: (i, 0)),
        out_shape=jax.ShapeDtypeStruct((rows_total, width), BF16),
        scratch_shapes=[pltpu.VMEM((n_pairs, 2, batch, PAIR_STATE), F32),
                        pltpu.VMEM((width // LANES, rows, PAIRS_PER_TILE * PAIR_COLS), F32),
                        pltpu.VMEM((width // LANES, rows, PAIRS_PER_TILE * PAIR_COLS), F32)],
        compiler_params=pltpu.CompilerParams(
            dimension_semantics=("arbitrary",), vmem_limit_bytes=VMEM_LIMIT_BYTES),
        name="ssm",
    )(u_tm, bpad, ar_v, ai_v, cpad, d_skip)


def _ssm_params(lam_re, lam_im, log_dt, b_re, b_im, c_re, c_im):
    g = lam_re.shape[0]
    dt = jnp.exp(log_dt.astype(F32))[:, None]
    lr = lam_re.astype(F32)
    li = lam_im.astype(F32)
    mag = jnp.exp(lr * dt)
    ar = mag * jnp.cos(li * dt)
    ai = mag * jnp.sin(li * dt)
    den = lr * lr + li * li
    cr = ((ar - 1.0) * lr + ai * li) / den
    ci = (ai * lr - (ar - 1.0) * li) / den
    br = b_re.astype(F32)
    bi = b_im.astype(F32)
    bbar_re = cr[..., None] * br - ci[..., None] * bi
    bbar_im = cr[..., None] * bi + ci[..., None] * br

    n_pairs = g // 2
    n_tiles = n_pairs // PAIRS_PER_TILE
    eye_pt = jnp.eye(PAIRS_PER_TILE, dtype=F32)
    eye_2 = jnp.eye(2, dtype=F32)
    bb = jnp.stack([bbar_re, bbar_im], axis=0).reshape(2, n_tiles, PAIRS_PER_TILE, 2, SSM_STATE, SSM_GROUP)
    bpad = jnp.einsum('aqlgpc,lm,gh->qlmgcahp', bb, eye_pt, eye_2)
    bpad = bpad.reshape(n_pairs, LANES, PAIR_COLS)
    cc = jnp.stack([c_re.astype(F32), -c_im.astype(F32)], axis=0)
    cc = cc.reshape(2, n_tiles, PAIRS_PER_TILE, 2, SSM_GROUP, SSM_STATE)
    cpad = jnp.einsum('aqlgcp,lm,gh->qlagpmhc', cc, eye_pt, eye_2)
    cpad = cpad.reshape(n_tiles, PAIRS_PER_TILE * PAIR_COLS, LANES)
    ar_v = ar.reshape(n_pairs, 1, PAIR_STATE)
    ai_v = ai.reshape(n_pairs, 1, PAIR_STATE)
    return bpad.astype(BF16), ar_v, ai_v, cpad.astype(BF16)


def _alibi_slopes():
    i = np.arange(1, N_ATTN_HEADS + 1, dtype=np.float32)
    s = np.exp2(-8.0 * i / N_ATTN_HEADS).astype(np.float32)
    return s.reshape(HEADS_PER_PATTERN, N_PATTERNS).T


def _attn_bias():
    slopes = _alibi_slopes()
    qi = np.arange(BLOCK)[:, None]
    ki = np.arange(2 * BLOCK)[None, :]
    delta = qi + BLOCK - ki
    heads = LANES // HEAD_DIM
    out = np.empty((N_PATTERNS, HEADS_PER_PATTERN // heads, heads * BLOCK, 2 * BLOCK), np.float32)
    for g, (window, dilation) in enumerate(ATTN_PATTERNS):
        reach = window // dilation
        valid = (delta >= 0) & (delta <= reach)
        dist = (delta * dilation).astype(np.float32)
        for h in range(HEADS_PER_PATTERN):
            rows = slice((h % heads) * BLOCK, (h % heads + 1) * BLOCK)
            out[g, h // heads, rows] = np.where(valid, -slopes[g, h] * dist * LOG2E, -np.inf)
    return out


def _attn_kernel(q0, k0, v0, q1, k1, v1, q2, k2, v2, bias_ref, o_ref,
                 og0, og1, og2, dg0, dg1, dg2, mg0, mg1, mg2):
    qkv = ((q0, k0, v0), (q1, k1, v1), (q2, k2, v2))
    ogs = (og0, og1, og2)
    dgs = (dg0, dg1, dg2)
    mgs = (mg0, mg1, mg2)
    seq = q0.shape[0]
    n_blocks = seq // BLOCK
    lane = lax.broadcasted_iota(jnp.int32, (BLOCK, LANES), 1)
    head0 = lane < HEAD_DIM

    blocks = [(g, bi) for g in range(N_PATTERNS) for bi in range(n_blocks)]

    def key_rows(g, bi):
        nb = n_blocks // ATTN_PATTERNS[g][1]
        first = bi % nb == 0
        return slice((bi if first else bi - 1) * BLOCK, (bi + 1) * BLOCK), first

    def logits(g, bi):
        q_ref, k_ref, _ = qkv[g]
        keys, first = key_rows(g, bi)
        q = q_ref[bi * BLOCK:(bi + 1) * BLOCK, :]
        zero = jnp.zeros_like(q)
        qs = jnp.concatenate([jnp.where(head0, q, zero), jnp.where(head0, zero, q)], axis=0)
        s = lax.dot_general(qs, k_ref[keys, :], (((1,), (1,)), ((), ())), preferred_element_type=F32)
        return s + (bias_ref[g, :, BLOCK:] if first else bias_ref[g])

    def softmax_numerator(s):
        m = jnp.max(s, axis=-1, keepdims=True)
        return jnp.exp2(s - m).astype(BF16), m

    def weighted_values(g, bi, p, m):
        _, _, v_ref = qkv[g]
        dilation = ATTN_PATTERNS[g][1]
        nb = n_blocks // dilation
        keys, _ = key_rows(g, bi)
        v = v_ref[keys, :]
        pv = jnp.dot(p, jnp.concatenate([v, jnp.ones_like(v)], axis=1), preferred_element_type=F32)
        out_start = bi // nb + (bi % nb) * BLOCK * dilation
        dst = pl.ds(out_start, BLOCK) if dilation == 1 else pl.ds(out_start, BLOCK, stride=dilation)
        ogs[g][dst, :] = jnp.where(head0, pv[:BLOCK, :LANES], pv[BLOCK:, :LANES])
        dgs[g][dst, :] = jnp.where(head0, pv[:BLOCK, LANES:], pv[BLOCK:, LANES:])
        mgs[g][dst, :] = jnp.where(head0, jnp.broadcast_to(m[:BLOCK], (BLOCK, LANES)),
                                   jnp.broadcast_to(m[BLOCK:], (BLOCK, LANES)))

    scores, numer = {}, {}
    for step in range(len(blocks) + 2 * ATTN_LAG):
        if step < len(blocks):
            scores[step] = logits(*blocks[step])
        if 0 <= step - ATTN_LAG < len(blocks):
            numer[step - ATTN_LAG] = softmax_numerator(scores.pop(step - ATTN_LAG))
        if 0 <= step - 2 * ATTN_LAG < len(blocks):
            weighted_values(*blocks[step - 2 * ATTN_LAG], *numer.pop(step - 2 * ATTN_LAG))

    chunk = 2 * BLOCK
    for c in range(0, seq, chunk):
        rows = pl.ds(c, chunk)
        ms = [mg[rows, :] for mg in mgs]
        mx = jnp.maximum(jnp.maximum(ms[0], ms[1]), ms[2])
        es = [jnp.exp2(m - mx) for m in ms]
        tot = es[0] * dgs[0][rows, :] + es[1] * dgs[1][rows, :] + es[2] * dgs[2][rows, :]
        acc = es[0] * ogs[0][rows, :] + es[1] * ogs[1][rows, :] + es[2] * ogs[2][rows, :]
        o_ref[rows, :] = (acc * (1.0 / tot)).astype(BF16)


def _attention(qkv, bias):
    b, seq, width = qkv[0].shape
    col_blocks = width // LANES
    tile = pl.BlockSpec((None, seq, LANES), lambda j, i: (i, 0, j))
    bias_spec = pl.BlockSpec((N_PATTERNS, None) + bias.shape[2:], lambda j, i: (0, j, 0, 0))
    return pl.pallas_call(
        _attn_kernel,
        grid=(col_blocks, b),
        in_specs=[tile] * 9 + [bias_spec],
        out_specs=tile,
        out_shape=jax.ShapeDtypeStruct((b, seq, width), BF16),
        scratch_shapes=[pltpu.VMEM((seq, LANES), F32)] * 9,
        compiler_params=pltpu.CompilerParams(
            dimension_semantics=("arbitrary", "arbitrary"), vmem_limit_bytes=VMEM_LIMIT_BYTES),
        name="attn",
    )(*qkv, bias)


def _mix_kernel(x_ref, ys_ref, oa_ref, ga_ref, gb_ref, wv_ref, wg_ref, wup_ref, wout_ref, o_ref):
    ys = ys_ref[...]
    val = jnp.dot(ys, wv_ref[...], preferred_element_type=F32)
    gate = jnp.dot(ys, wg_ref[...], preferred_element_type=F32)
    y_a = val * jax.nn.sigmoid(gate)
    y_b = jnp.dot(oa_ref[...], wup_ref[...], preferred_element_type=F32)
    merged = ga_ref[...].astype(F32) * y_a + gb_ref[...].astype(F32) * y_b
    o_ref[...] = x_ref[...] + jnp.dot(merged.astype(BF16), wout_ref[...], preferred_element_type=F32)


def _mix(x2, ys, oa, ga, gb, wv, wg, wup, wout):
    t, d = x2.shape
    row = lambda width: pl.BlockSpec((MIX_ROWS, width), lambda i: (i, 0))
    return pl.pallas_call(
        _mix_kernel,
        grid=(t // MIX_ROWS,),
        in_specs=[row(d), row(ys.shape[1]), row(oa.shape[1]), row(d), row(d),
                  _resident(wv.shape), _resident(wg.shape), _resident(wup.shape), _resident(wout.shape)],
        out_specs=row(d),
        out_shape=jax.ShapeDtypeStruct((t, d), F32),
        compiler_params=pltpu.CompilerParams(
            dimension_semantics=("arbitrary",), vmem_limit_bytes=VMEM_LIMIT_BYTES),
        name="mix",
    )(x2, ys, oa, ga, gb, wv, wg, wup, wout)


def _ffn_kernel(x_ref, g_ref, wg_ref, wu_ref, cw_ref, cb_ref, wd_ref, o_ref, carry_ref, gs_ref, h_ref):
    rows = x_ref.shape[0]
    halo = SUBLANES

    @pl.when(pl.program_id(1) == 0)
    def _():
        carry_ref[...] = jnp.zeros_like(carry_ref)

    x = x_ref[...]
    ms = jnp.mean(x * x, axis=-1, keepdims=True)
    hf = (x * lax.rsqrt(ms + EPS) * g_ref[...]).astype(BF16)
    d_ff = wg_ref.shape[1]
    for c in range(0, d_ff, FFN_COLS):
        cols = slice(c, c + FFN_COLS)
        gate = jnp.dot(hf, wg_ref[:, cols], preferred_element_type=F32)
        up = jnp.dot(hf, wu_ref[:, cols], preferred_element_type=F32)
        gs_ref[0:halo, :] = carry_ref[:, cols]
        gs_ref[halo:halo + rows, :] = gate
        carry_ref[:, cols] = gate[rows - halo:rows, :]
        pre = cb_ref[:, cols] + cw_ref[CONV_WIDTH - 1:CONV_WIDTH, cols] * gate
        for tap in range(1, CONV_WIDTH):
            shifted = gs_ref[halo - tap:halo - tap + rows, :]
            pre = pre + cw_ref[CONV_WIDTH - 1 - tap:CONV_WIDTH - tap, cols] * shifted
        h_ref[:, cols] = (_gelu(pre) * up).astype(BF16)
    o_ref[...] = x + jnp.dot(h_ref[...], wd_ref[...], preferred_element_type=F32)


def _ffn(x3, norm_g, wg, wu, conv_w, conv_b, wd):
    b, seq, d = x3.shape
    d_ff = wg.shape[1]
    tile = pl.BlockSpec((None, FFN_ROWS, d), lambda i, j: (i, j, 0))
    return pl.pallas_call(
        _ffn_kernel,
        grid=(b, seq // FFN_ROWS),
        in_specs=[tile, _resident((1, d)), _resident(wg.shape), _resident(wu.shape),
                  _resident(conv_w.shape), _resident((1, d_ff)), _resident(wd.shape)],
        out_specs=tile,
        out_shape=jax.ShapeDtypeStruct((b, seq, d), F32),
        scratch_shapes=[pltpu.VMEM((SUBLANES, d_ff), F32),
                        pltpu.VMEM((FFN_ROWS + SUBLANES, FFN_COLS), F32),
                        pltpu.VMEM((FFN_ROWS, d_ff), BF16)],
        compiler_params=pltpu.CompilerParams(
            dimension_semantics=("arbitrary", "arbitrary"), vmem_limit_bytes=VMEM_LIMIT_BYTES),
        name="ffn",
    )(x3, norm_g, wg, wu, conv_w, conv_b, wd)


def _layer(x, norm_mix_g, w_in, lam_re, lam_im, log_dt, b_re, b_im, c_re, c_im, ssm_d,
           glu_w_val, glu_w_gate, q_norm_g, k_norm_g, w_attn_up, w_out, norm_ffn_g,
           ffn_w_gate, ffn_w_up, ffn_conv_w, ffn_conv_b, ffn_w_down):
    b, seq, d = x.shape
    t = b * seq
    x2 = x.reshape(t, d)
    scale = HEAD_DIM ** -0.5 * LOG2E
    q_gain = jnp.tile(q_norm_g.astype(F32) * scale, HEADS_PER_PATTERN)[None, :]
    k_gain = jnp.tile(k_norm_g.astype(F32), HEADS_PER_PATTERN)[None, :]

    outs = _in_proj(x, norm_mix_g.astype(F32)[None, :], w_in.astype(BF16), q_gain, k_gain)
    u, qkv, gate_a, gate_b = outs[0], outs[1:10], outs[10], outs[11]

    ssm_width = u.shape[-1]
    u_tm = u.transpose(1, 0, 2).reshape(t, ssm_width)
    bpad, ar_v, ai_v, cpad = _ssm_params(lam_re, lam_im, log_dt, b_re, b_im, c_re, c_im)
    ys_tm = _ssm(u_tm, bpad, ar_v, ai_v, cpad, ssm_d.astype(F32)[None, :], b)
    ys = ys_tm.reshape(seq, b, ssm_width).transpose(1, 0, 2).reshape(t, ssm_width)

    attn_in = [qkv[kind * N_PATTERNS + g].reshape(b, seq, PATTERN_WIDTH)
               for g in range(N_PATTERNS) for kind in range(3)]
    o_attn = _attention(attn_in, jnp.asarray(_attn_bias())).reshape(t, PATTERN_WIDTH)

    x1 = _mix(x2, ys, o_attn, gate_a.reshape(t, d), gate_b.reshape(t, d), glu_w_val.astype(BF16),
              glu_w_gate.astype(BF16), w_attn_up.astype(BF16), w_out.astype(BF16))

    out = _ffn(x1.reshape(b, seq, d), norm_ffn_g.astype(F32)[None, :], ffn_w_gate.astype(BF16),
               ffn_w_up.astype(BF16), ffn_conv_w.astype(F32), ffn_conv_b.astype(F32)[None, :],
               ffn_w_down.astype(BF16))
    return out


def kernel(x, norm_mix_g, w_in, ssm_lambda_re, ssm_lambda_im, ssm_log_dt, ssm_b_re, ssm_b_im, ssm_c_re, ssm_c_im, ssm_d, glu_w_val, glu_w_gate, q_norm_g, k_norm_g, w_attn_up, w_out, norm_ffn_g, ffn_w_gate, ffn_w_up, ffn_conv_w, ffn_conv_b, ffn_w_down):
    depth = w_in.shape[0]
    for i in range(depth):
        x = _layer(x, norm_mix_g[i], w_in[i], ssm_lambda_re[i], ssm_lambda_im[i], ssm_log_dt[i],
                   ssm_b_re[i], ssm_b_im[i], ssm_c_re[i], ssm_c_im[i], ssm_d[i], glu_w_val[i],
                   glu_w_gate[i], q_norm_g[i], k_norm_g[i], w_attn_up[i], w_out[i], norm_ffn_g[i],
                   ffn_w_gate[i], ffn_w_up[i], ffn_conv_w[i], ffn_conv_b[i], ffn_w_down[i]).astype(x.dtype)
    return x
```

```python
import functools
import math

import numpy as np
import jax
import jax.numpy as jnp
from jax import lax
from jax.experimental import pallas as pl
from jax.experimental.pallas import tpu as pltpu

F32 = jnp.float32
BF16 = jnp.bfloat16

SSM_GROUP = 16
SSM_STATE = 64
HEAD_DIM = 64
ATTN_PATTERNS = ((128, 1), (512, 4), (2048, 16))
N_PATTERNS = len(ATTN_PATTERNS)
HEADS_PER_PATTERN = 8
N_ATTN_HEADS = N_PATTERNS * HEADS_PER_PATTERN
PATTERN_WIDTH = HEADS_PER_PATTERN * HEAD_DIM
BLOCK = 128
CONV_WIDTH = 3
EPS = 1e-6
LOG2E = math.log2(math.e)

LANES = 128
SUBLANES = 8
VMEM_LIMIT_BYTES = 56 * 1024 * 1024

IN_ROWS = 512
MIX_ROWS = 512
FFN_ROWS = 512
FFN_COLS = 256
SSM_STEPS = 32
ATTN_LAG = 2
GROUPS_PER_TILE = LANES // SSM_GROUP
PAIRS_PER_TILE = GROUPS_PER_TILE // 2
PAIR_STATE = 2 * SSM_STATE
PAIR_COLS = 2 * PAIR_STATE


def _resident(shape):
    nd = len(shape)
    return pl.BlockSpec(shape, lambda *_: (0,) * nd, pipeline_mode=pl.Buffered(1))


def _gelu(z):
    return 0.5 * z * (1.0 + lax.erf(z * (1.0 / math.sqrt(2.0))))


def _in_proj_kernel(x_ref, g_ref, w_ref, qg_ref, kg_ref, ones_ref,
                    u_ref, qkv0, qkv1, qkv2, gates_ref, slab_ref, h_ref):
    rows, d = x_ref.shape
    n_slabs = d // LANES
    x = x_ref[...]
    ms = jnp.mean(x * x, axis=-1, keepdims=True)
    hn = x * lax.rsqrt(ms + EPS) * g_ref[...]
    h_ref[0] = hn.astype(BF16)
    for s in range(n_slabs):
        slab_ref[0, s] = hn[:, s * LANES:(s + 1) * LANES]

    prev = 1
    for g in range(1, N_PATTERNS):
        dilation = ATTN_PATTERNS[g][1]
        factor = dilation // prev
        n_prev, n = rows // prev, rows // dilation
        for s in range(n_slabs):
            for r_prev in range(prev):
                for a in range(factor):
                    piece = slab_ref[g - 1, s, pl.ds(r_prev * n_prev + a, n, stride=factor), :]
                    r = a * prev + r_prev
                    h_ref[g, r * n:(r + 1) * n, s * LANES:(s + 1) * LANES] = piece.astype(BF16)
                    if g + 1 < N_PATTERNS:
                        slab_ref[g, s, r * n:(r + 1) * n, :] = piece
        prev = dilation

    def proj(g, col0, width):
        return jnp.dot(h_ref[g], w_ref[:, col0:col0 + width], preferred_element_type=F32)

    def head_norm(z, gain):
        sq = (z * z).astype(BF16)
        half = ones_ref.shape[0]
        parts = [jnp.dot(sq[:, c:c + half], ones_ref[...], preferred_element_type=F32)
                 for c in range(0, z.shape[1], half)]
        ssq = jnp.concatenate(parts, axis=1)
        return z * lax.rsqrt(ssq * (1.0 / HEAD_DIM) + EPS) * gain

    def store(ref, z, kind):
        for j in range(z.shape[1] // LANES):
            cols = slice((3 * j + kind) * LANES, (3 * j + kind + 1) * LANES)
            piece = z[:, j * LANES:(j + 1) * LANES].astype(BF16)
            if len(ref.shape) == 2:
                ref[:, cols] = piece
            else:
                n = ref.shape[1]
                for r in range(ref.shape[0]):
                    ref[r, :, cols] = piece[r * n:(r + 1) * n]

    w = PATTERN_WIDTH
    u_ref[...] = proj(0, 0, w)
    col = w
    for kind, gain_ref in enumerate((qg_ref, kg_ref, None)):
        for g, ref in enumerate((qkv0, qkv1, qkv2)):
            z = proj(g, col, w)
            store(ref, z if gain_ref is None else head_norm(z, gain_ref[...]), kind)
            col += w
    for c in range(0, gates_ref.shape[1], w):
        gates_ref[:, c:c + w] = jax.nn.sigmoid(proj(0, col, w)).astype(BF16)
        col += w


def _in_proj(x, norm_g, w_in, q_gain, k_gain):
    b, seq, d = x.shape
    n_in = w_in.shape[1]
    w = PATTERN_WIDTH
    d_gate = (n_in - w - 3 * N_PATTERNS * w) // 2
    ones_bd = jnp.asarray(np.kron(np.eye(256 // HEAD_DIM), np.ones((HEAD_DIM, HEAD_DIM))), BF16)
    row = lambda width: pl.BlockSpec((None, IN_ROWS, width), lambda i, j: (i, j, 0))

    def pattern_spec(dilation):
        if dilation == 1:
            return row(3 * w), jax.ShapeDtypeStruct((b, seq, 3 * w), BF16)
        spec = pl.BlockSpec((None, dilation, IN_ROWS // dilation, 3 * w), lambda i, j: (i, 0, j, 0))
        return spec, jax.ShapeDtypeStruct((b, dilation, seq // dilation, 3 * w), BF16)

    qkv = [pattern_spec(dilation) for _, dilation in ATTN_PATTERNS]
    out_specs = [row(w)] + [s for s, _ in qkv] + [row(2 * d_gate)]
    out_shape = ([jax.ShapeDtypeStruct((b, seq, w), F32)] + [s for _, s in qkv]
                 + [jax.ShapeDtypeStruct((b, seq, 2 * d_gate), BF16)])
    return pl.pallas_call(
        _in_proj_kernel,
        grid=(b, seq // IN_ROWS),
        in_specs=[row(d), _resident((1, d)), _resident(w_in.shape),
                  _resident((1, w)), _resident((1, w)), _resident(ones_bd.shape)],
        out_specs=out_specs,
        out_shape=out_shape,
        scratch_shapes=[pltpu.VMEM((N_PATTERNS - 1, d // LANES, IN_ROWS, LANES), F32),
                        pltpu.VMEM((N_PATTERNS, IN_ROWS, d), BF16)],
        compiler_params=pltpu.CompilerParams(
            dimension_semantics=("arbitrary", "arbitrary"), vmem_limit_bytes=VMEM_LIMIT_BYTES),
        name="in_proj",
    )(x, norm_g, w_in, q_gain, k_gain, ones_bd)


def _ssm_kernel(u_ref, bp_ref, ar_ref, ai_ref, cp_ref, d_ref, y_ref, state_ref, bu_ref, xs_ref, *, batch):
    @pl.when(pl.program_id(0) == 0)
    def _():
        state_ref[...] = jnp.zeros_like(state_ref)

    n_tiles = u_ref.shape[1] // LANES
    for q in range(n_tiles):
        u_tile = u_ref[:, q * LANES:(q + 1) * LANES]
        ub = u_tile.astype(BF16)
        for gl in range(PAIRS_PER_TILE):
            gp = q * PAIRS_PER_TILE + gl
            bu_ref[q, :, gl * PAIR_COLS:(gl + 1) * PAIR_COLS] = jnp.dot(
                ub, bp_ref[gp], preferred_element_type=F32)

        for gl in range(PAIRS_PER_TILE):
            gp = q * PAIRS_PER_TILE + gl
            a_re, a_im = ar_ref[gp], ai_ref[gp]
            xr, xi = state_ref[gp, 0], state_ref[gp, 1]
            c0 = gl * PAIR_COLS
            for t in range(SSM_STEPS):
                rows = slice(t * batch, (t + 1) * batch)
                br = bu_ref[q, rows, c0:c0 + PAIR_STATE]
                bi = bu_ref[q, rows, c0 + PAIR_STATE:c0 + PAIR_COLS]
                xr, xi = a_re * xr - a_im * xi + br, a_re * xi + a_im * xr + bi
                xs_ref[q, rows, c0:c0 + PAIR_STATE] = xr
                xs_ref[q, rows, c0 + PAIR_STATE:c0 + PAIR_COLS] = xi
            state_ref[gp, 0] = xr
            state_ref[gp, 1] = xi

        y = jnp.dot(xs_ref[q].astype(BF16), cp_ref[q], preferred_element_type=F32)
        y = y + d_ref[:, q * LANES:(q + 1) * LANES] * u_tile
        y_ref[:, q * LANES:(q + 1) * LANES] = _gelu(y).astype(BF16)


def _ssm(u_tm, bpad, ar_v, ai_v, cpad, d_skip, batch):
    rows_total, width = u_tm.shape
    rows = SSM_STEPS * batch
    n_pairs = bpad.shape[0]
    kernel = functools.partial(_ssm_kernel, batch=batch)
    return pl.pallas_call(
        kernel,
        grid=(rows_total // rows,),
        in_specs=[pl.BlockSpec((rows, width), lambda i: (i, 0)),
                  _resident(bpad.shape), _resident(ar_v.shape), _resident(ai_v.shape),
                  _resident(cpad.shape), _resident(d_skip.shape)],
        out_specs=pl.BlockSpec((rows, width), lambda i: (i, 0)),
        out_shape=jax.ShapeDtypeStruct((rows_total, width), BF16),
        scratch_shapes=[pltpu.VMEM((n_pairs, 2, batch, PAIR_STATE), F32),
                        pltpu.VMEM((width // LANES, rows, PAIRS_PER_TILE * PAIR_COLS), F32),
                        pltpu.VMEM((width // LANES, rows, PAIRS_PER_TILE * PAIR_COLS), F32)],
        compiler_params=pltpu.CompilerParams(
            dimension_semantics=("arbitrary",), vmem_limit_bytes=VMEM_LIMIT_BYTES),
        name="ssm",
    )(u_tm, bpad, ar_v, ai_v, cpad, d_skip)


def _ssm_params(lam_re, lam_im, log_dt, b_re, b_im, c_re, c_im):
    g = lam_re.shape[0]
    dt = jnp.exp(log_dt.astype(F32))[:, None]
    lr = lam_re.astype(F32)
    li = lam_im.astype(F32)
    mag = jnp.exp(lr * dt)
    ar = mag * jnp.cos(li * dt)
    ai = mag * jnp.sin(li * dt)
    den = lr * lr + li * li
    cr = ((ar - 1.0) * lr + ai * li) / den
    ci = (ai * lr - (ar - 1.0) * li) / den
    br = b_re.astype(F32)
    bi = b_im.astype(F32)
    bbar_re = cr[..., None] * br - ci[..., None] * bi
    bbar_im = cr[..., None] * bi + ci[..., None] * br

    n_pairs = g // 2
    n_tiles = n_pairs // PAIRS_PER_TILE
    eye_pt = jnp.eye(PAIRS_PER_TILE, dtype=F32)
    eye_2 = jnp.eye(2, dtype=F32)
    bb = jnp.stack([bbar_re, bbar_im], axis=0).reshape(2, n_tiles, PAIRS_PER_TILE, 2, SSM_STATE, SSM_GROUP)
    bpad = jnp.einsum('aqlgpc,lm,gh->qlmgcahp', bb, eye_pt, eye_2)
    bpad = bpad.reshape(n_pairs, LANES, PAIR_COLS)
    cc = jnp.stack([c_re.astype(F32), -c_im.astype(F32)], axis=0)
    cc = cc.reshape(2, n_tiles, PAIRS_PER_TILE, 2, SSM_GROUP, SSM_STATE)
    cpad = jnp.einsum('aqlgcp,lm,gh->qlagpmhc', cc, eye_pt, eye_2)
    cpad = cpad.reshape(n_tiles, PAIRS_PER_TILE * PAIR_COLS, LANES)
    ar_v = ar.reshape(n_pairs, 1, PAIR_STATE)
    ai_v = ai.reshape(n_pairs, 1, PAIR_STATE)
    return bpad.astype(BF16), ar_v, ai_v, cpad.astype(BF16)


def _alibi_slopes():
    i = np.arange(1, N_ATTN_HEADS + 1, dtype=np.float32)
    s = np.exp2(-8.0 * i / N_ATTN_HEADS).astype(np.float32)
    return s.reshape(HEADS_PER_PATTERN, N_PATTERNS).T


def _attn_bias():
    slopes = _alibi_slopes()
    qi = np.arange(BLOCK)[:, None]
    ki = np.arange(2 * BLOCK)[None, :]
    delta = qi + BLOCK - ki
    heads = LANES // HEAD_DIM
    out = np.empty((N_PATTERNS, HEADS_PER_PATTERN // heads, heads * BLOCK, 2 * BLOCK), np.float32)
    for g, (window, dilation) in enumerate(ATTN_PATTERNS):
        reach = window // dilation
        valid = (delta >= 0) & (delta <= reach)
        dist = (delta * dilation).astype(np.float32)
        for h in range(HEADS_PER_PATTERN):
            rows = slice((h % heads) * BLOCK, (h % heads + 1) * BLOCK)
            out[g, h // heads, rows] = np.where(valid, -slopes[g, h] * dist * LOG2E, -np.inf)
    return out


def _attn_kernel(qkv0, qkv1, qkv2, bias_ref, o_ref, og0, og1, og2, dg0, dg1, dg2, mg0, mg1, mg2):
    qkv = (qkv0, qkv1, qkv2)
    ogs = (og0, og1, og2)
    dgs = (dg0, dg1, dg2)
    mgs = (mg0, mg1, mg2)
    seq = qkv0.shape[0]
    n_blocks = seq // BLOCK
    q_cols, k_cols, v_cols = (slice(i * LANES, (i + 1) * LANES) for i in range(3))
    lane = lax.broadcasted_iota(jnp.int32, (BLOCK, LANES), 1)
    head0 = lane < HEAD_DIM

    blocks = [(g, bi) for g in range(N_PATTERNS) for bi in range(n_blocks)]

    def is_first(g, bi):
        return bi % (n_blocks // ATTN_PATTERNS[g][1]) == 0

    def split_heads(a):
        lanes = lax.broadcasted_iota(jnp.int32, a.shape, 1)
        keep = (lanes & (LANES - 1)) < HEAD_DIM
        zero = jnp.zeros_like(a)
        return jnp.concatenate([jnp.where(keep, a, zero), jnp.where(keep, zero, a)], axis=0)

    def logits(g, bi):
        ref = qkv[g]
        q = ref[bi * BLOCK:(bi + 1) * BLOCK, q_cols]
        contract_lanes = (((1,), (1,)), ((), ()))
        if is_first(g, bi):
            k = split_heads(ref[bi * BLOCK:(bi + 1) * BLOCK, k_cols])
            s = lax.dot_general(q, k, contract_lanes, preferred_element_type=F32)
            bias = jnp.concatenate([bias_ref[g, :BLOCK, BLOCK:], bias_ref[g, BLOCK:, BLOCK:]], axis=1)
            return s + bias
        k = ref[(bi - 1) * BLOCK:(bi + 1) * BLOCK, k_cols]
        s = lax.dot_general(split_heads(q), k, contract_lanes, preferred_element_type=F32)
        return s + bias_ref[g]

    def softmax_numerator(g, bi, s):
        if is_first(g, bi):
            m0 = jnp.max(s[:, :BLOCK], axis=-1, keepdims=True)
            m1 = jnp.max(s[:, BLOCK:], axis=-1, keepdims=True)
            p = jnp.concatenate([jnp.exp2(s[:, :BLOCK] - m0), jnp.exp2(s[:, BLOCK:] - m1)], axis=1)
            return p.astype(BF16), (m0, m1)
        m = jnp.max(s, axis=-1, keepdims=True)
        return jnp.exp2(s - m).astype(BF16), (m[:BLOCK], m[BLOCK:])

    def weighted_values(g, bi, p, m):
        ref = qkv[g]
        dilation = ATTN_PATTERNS[g][1]
        nb = n_blocks // dilation
        out_start = bi // nb + (bi % nb) * BLOCK * dilation
        dst = pl.ds(out_start, BLOCK) if dilation == 1 else pl.ds(out_start, BLOCK, stride=dilation)
        if is_first(g, bi):
            v = ref[bi * BLOCK:(bi + 1) * BLOCK, v_cols]
            pv = jnp.dot(p, split_heads(jnp.concatenate([v, jnp.ones_like(v)], axis=1)),
                         preferred_element_type=F32)
            ogs[g][dst, :] = pv[:, :LANES]
            dgs[g][dst, :] = pv[:, LANES:]
        else:
            v = ref[(bi - 1) * BLOCK:(bi + 1) * BLOCK, v_cols]
            pv = jnp.dot(p, jnp.concatenate([v, jnp.ones_like(v)], axis=1), preferred_element_type=F32)
            ogs[g][dst, :] = jnp.where(head0, pv[:BLOCK, :LANES], pv[BLOCK:, :LANES])
            dgs[g][dst, :] = jnp.where(head0, pv[:BLOCK, LANES:], pv[BLOCK:, LANES:])
        mgs[g][dst, :] = jnp.where(head0, jnp.broadcast_to(m[0], (BLOCK, LANES)),
                                   jnp.broadcast_to(m[1], (BLOCK, LANES)))

    scores, numer = {}, {}
    for step in range(len(blocks) + 2 * ATTN_LAG):
        if step < len(blocks):
            scores[step] = logits(*blocks[step])
        if 0 <= step - ATTN_LAG < len(blocks):
            numer[step - ATTN_LAG] = softmax_numerator(*blocks[step - ATTN_LAG], scores.pop(step - ATTN_LAG))
        if 0 <= step - 2 * ATTN_LAG < len(blocks):
            weighted_values(*blocks[step - 2 * ATTN_LAG], *numer.pop(step - 2 * ATTN_LAG))

    chunk = 2 * BLOCK
    for c in range(0, seq, chunk):
        rows = pl.ds(c, chunk)
        ms = [mg[rows, :] for mg in mgs]
        mx = jnp.maximum(jnp.maximum(ms[0], ms[1]), ms[2])
        es = [jnp.exp2(m - mx) for m in ms]
        tot = es[0] * dgs[0][rows, :] + es[1] * dgs[1][rows, :] + es[2] * dgs[2][rows, :]
        acc = es[0] * ogs[0][rows, :] + es[1] * ogs[1][rows, :] + es[2] * ogs[2][rows, :]
        o_ref[rows, :] = (acc * (1.0 / tot)).astype(BF16)


def _attention(qkv, bias):
    b, seq, width = qkv[0].shape
    col_blocks = width // (3 * LANES)
    window = pl.BlockSpec((None, seq, 3 * LANES), lambda j, i: (i, 0, j))
    bias_spec = pl.BlockSpec((N_PATTERNS, None) + bias.shape[2:], lambda j, i: (0, j, 0, 0))
    return pl.pallas_call(
        _attn_kernel,
        grid=(col_blocks, b),
        in_specs=[window] * N_PATTERNS + [bias_spec],
        out_specs=pl.BlockSpec((None, seq, LANES), lambda j, i: (i, 0, j)),
        out_shape=jax.ShapeDtypeStruct((b, seq, col_blocks * LANES), BF16),
        scratch_shapes=[pltpu.VMEM((seq, LANES), F32)] * 9,
        compiler_params=pltpu.CompilerParams(
            dimension_semantics=("arbitrary", "arbitrary"), vmem_limit_bytes=VMEM_LIMIT_BYTES),
        name="attn",
    )(*qkv, bias)


def _mix_kernel(x_ref, ys_ref, oa_ref, gates_ref, wv_ref, wg_ref, wup_ref, wout_ref, o_ref):
    ys = ys_ref[...]
    val = jnp.dot(ys, wv_ref[...], preferred_element_type=F32)
    gate = jnp.dot(ys, wg_ref[...], preferred_element_type=F32)
    y_a = val * jax.nn.sigmoid(gate)
    y_b = jnp.dot(oa_ref[...], wup_ref[...], preferred_element_type=F32)
    d = x_ref.shape[1]
    merged = gates_ref[:, :d].astype(F32) * y_a + gates_ref[:, d:].astype(F32) * y_b
    o_ref[...] = x_ref[...] + jnp.dot(merged.astype(BF16), wout_ref[...], preferred_element_type=F32)


def _mix(x2, ys, oa, gates, wv, wg, wup, wout):
    t, d = x2.shape
    row = lambda width: pl.BlockSpec((MIX_ROWS, width), lambda i: (i, 0))
    return pl.pallas_call(
        _mix_kernel,
        grid=(t // MIX_ROWS,),
        in_specs=[row(d), row(ys.shape[1]), row(oa.shape[1]), row(gates.shape[1]),
                  _resident(wv.shape), _resident(wg.shape), _resident(wup.shape), _resident(wout.shape)],
        out_specs=row(d),
        out_shape=jax.ShapeDtypeStruct((t, d), F32),
        compiler_params=pltpu.CompilerParams(
            dimension_semantics=("arbitrary",), vmem_limit_bytes=VMEM_LIMIT_BYTES),
        name="mix",
    )(x2, ys, oa, gates, wv, wg, wup, wout)


def _ffn_kernel(x_ref, g_ref, wg_ref, wu_ref, cw_ref, cb_ref, wd_ref, o_ref, carry_ref, gs_ref, h_ref):
    rows = x_ref.shape[0]
    halo = SUBLANES

    @pl.when(pl.program_id(1) == 0)
    def _():
        carry_ref[...] = jnp.zeros_like(carry_ref)

    x = x_ref[...]
    ms = jnp.mean(x * x, axis=-1, keepdims=True)
    hf = (x * lax.rsqrt(ms + EPS) * g_ref[...]).astype(BF16)
    d_ff = wg_ref.shape[1]
    for c in range(0, d_ff, FFN_COLS):
        cols = slice(c, c + FFN_COLS)
        gate = jnp.dot(hf, wg_ref[:, cols], preferred_element_type=F32)
        up = jnp.dot(hf, wu_ref[:, cols], preferred_element_type=F32)
        gs_ref[0:halo, :] = carry_ref[:, cols]
        gs_ref[halo:halo + rows, :] = gate
        carry_ref[:, cols] = gate[rows - halo:rows, :]
        pre = cb_ref[:, cols] + cw_ref[CONV_WIDTH - 1:CONV_WIDTH, cols] * gate
        for tap in range(1, CONV_WIDTH):
            shifted = gs_ref[halo - tap:halo - tap + rows, :]
            pre = pre + cw_ref[CONV_WIDTH - 1 - tap:CONV_WIDTH - tap, cols] * shifted
        h_ref[:, cols] = (_gelu(pre) * up).astype(BF16)
    o_ref[...] = x + jnp.dot(h_ref[...], wd_ref[...], preferred_element_type=F32)


def _ffn(x3, norm_g, wg, wu, conv_w, conv_b, wd):
    b, seq, d = x3.shape
    d_ff = wg.shape[1]
    tile = pl.BlockSpec((None, FFN_ROWS, d), lambda i, j: (i, j, 0))
    return pl.pallas_call(
        _ffn_kernel,
        grid=(b, seq // FFN_ROWS),
        in_specs=[tile, _resident((1, d)), _resident(wg.shape), _resident(wu.shape),
                  _resident(conv_w.shape), _resident((1, d_ff)), _resident(wd.shape)],
        out_specs=tile,
        out_shape=jax.ShapeDtypeStruct((b, seq, d), F32),
        scratch_shapes=[pltpu.VMEM((SUBLANES, d_ff), F32),
                        pltpu.VMEM((FFN_ROWS + SUBLANES, FFN_COLS), F32),
                        pltpu.VMEM((FFN_ROWS, d_ff), BF16)],
        compiler_params=pltpu.CompilerParams(
            dimension_semantics=("arbitrary", "arbitrary"), vmem_limit_bytes=VMEM_LIMIT_BYTES),
        name="ffn",
    )(x3, norm_g, wg, wu, conv_w, conv_b, wd)


def _layer(x, norm_mix_g, w_in, lam_re, lam_im, log_dt, b_re, b_im, c_re, c_im, ssm_d,
           glu_w_val, glu_w_gate, q_norm_g, k_norm_g, w_attn_up, w_out, norm_ffn_g,
           ffn_w_gate, ffn_w_up, ffn_conv_w, ffn_conv_b, ffn_w_down):
    b, seq, d = x.shape
    t = b * seq
    x2 = x.reshape(t, d)
    scale = HEAD_DIM ** -0.5 * LOG2E
    q_gain = jnp.tile(q_norm_g.astype(F32) * scale, HEADS_PER_PATTERN)[None, :]
    k_gain = jnp.tile(k_norm_g.astype(F32), HEADS_PER_PATTERN)[None, :]

    outs = _in_proj(x, norm_mix_g.astype(F32)[None, :], w_in.astype(BF16), q_gain, k_gain)
    u, qkv, gates = outs[0], outs[1:1 + N_PATTERNS], outs[1 + N_PATTERNS]

    ssm_width = u.shape[-1]
    u_tm = u.transpose(1, 0, 2).reshape(t, ssm_width)
    bpad, ar_v, ai_v, cpad = _ssm_params(lam_re, lam_im, log_dt, b_re, b_im, c_re, c_im)
    ys_tm = _ssm(u_tm, bpad, ar_v, ai_v, cpad, ssm_d.astype(F32)[None, :], b)
    ys = ys_tm.reshape(seq, b, ssm_width).transpose(1, 0, 2).reshape(t, ssm_width)

    attn_in = [a.reshape(b, seq, 3 * PATTERN_WIDTH) for a in qkv]
    o_attn = _attention(attn_in, jnp.asarray(_attn_bias())).reshape(t, PATTERN_WIDTH)

    x1 = _mix(x2, ys, o_attn, gates.reshape(t, 2 * d), glu_w_val.astype(BF16), glu_w_gate.astype(BF16),
              w_attn_up.astype(BF16), w_out.astype(BF16))

    out = _ffn(x1.reshape(b, seq, d), norm_ffn_g.astype(F32)[None, :], ffn_w_gate.astype(BF16),
               ffn_w_up.astype(BF16), ffn_conv_w.astype(F32), ffn_conv_b.astype(F32)[None, :],
               ffn_w_down.astype(BF16))
    return out


def kernel(x, norm_mix_g, w_in, ssm_lambda_re, ssm_lambda_im, ssm_log_dt, ssm_b_re, ssm_b_im, ssm_c_re, ssm_c_im, ssm_d, glu_w_val, glu_w_gate, q_norm_g, k_norm_g, w_attn_up, w_out, norm_ffn_g, ffn_w_gate, ffn_w_up, ffn_conv_w, ffn_conv_b, ffn_w_down):
    depth = w_in.shape[0]
    for i in range(depth):
        x = _layer(x, norm_mix_g[i], w_in[i], ssm_lambda_re[i], ssm_lambda_im[i], ssm_log_dt[i],
                   ssm_b_re[i], ssm_b_im[i], ssm_c_re[i], ssm_c_im[i], ssm_d[i], glu_w_val[i],
                   glu_w_gate[i], q_norm_g[i], k_norm_g[i], w_attn_up[i], w_out[i], norm_ffn_g[i],
                   ffn_w_gate[i], ffn_w_up[i], ffn_conv_w[i], ffn_conv_b[i], ffn_w_down[i]).astype(x.dtype)
    return x
```

```python
import functools
import math

import numpy as np
import jax
import jax.numpy as jnp
from jax import lax
from jax.experimental import pallas as pl
from jax.experimental.pallas import tpu as pltpu

F32 = jnp.float32
BF16 = jnp.bfloat16

SSM_GROUP = 16
SSM_STATE = 64
HEAD_DIM = 64
ATTN_PATTERNS = ((128, 1), (512, 4), (2048, 16))
N_PATTERNS = len(ATTN_PATTERNS)
HEADS_PER_PATTERN = 8
N_ATTN_HEADS = N_PATTERNS * HEADS_PER_PATTERN
PATTERN_WIDTH = HEADS_PER_PATTERN * HEAD_DIM
BLOCK = 128
CONV_WIDTH = 3
EPS = 1e-6
LOG2E = math.log2(math.e)

LANES = 128
SUBLANES = 8
VMEM_LIMIT_BYTES = 56 * 1024 * 1024

IN_ROWS = 512
TAIL_ROWS = 512
FFN_COLS = 256
SSM_STEPS = 32
ATTN_GROUP = 1
ATTN_LAG = 2
GROUPS_PER_TILE = LANES // SSM_GROUP
PAIRS_PER_TILE = GROUPS_PER_TILE // 2
PAIR_STATE = 2 * SSM_STATE
PAIR_COLS = 2 * PAIR_STATE


def _resident(shape):
    nd = len(shape)
    return pl.BlockSpec(shape, lambda *_: (0,) * nd, pipeline_mode=pl.Buffered(1))


def _gelu(z):
    return 0.5 * z * (1.0 + lax.erf(z * (1.0 / math.sqrt(2.0))))


def _in_proj_kernel(x_ref, g_ref, w_ref, qg_ref, kg_ref, ones_ref,
                    u_ref, qkv0, qkv1, qkv2, gates_ref, slab_ref, h_ref):
    rows, d = x_ref.shape
    n_slabs = d // LANES
    x = x_ref[...]
    ms = jnp.mean(x * x, axis=-1, keepdims=True)
    hn = x * lax.rsqrt(ms + EPS) * g_ref[...]
    h_ref[0] = hn.astype(BF16)
    for s in range(n_slabs):
        slab_ref[0, s] = hn[:, s * LANES:(s + 1) * LANES]

    prev = 1
    for g in range(1, N_PATTERNS):
        dilation = ATTN_PATTERNS[g][1]
        factor = dilation // prev
        n_prev, n = rows // prev, rows // dilation
        for s in range(n_slabs):
            for r_prev in range(prev):
                for a in range(factor):
                    piece = slab_ref[g - 1, s, pl.ds(r_prev * n_prev + a, n, stride=factor), :]
                    r = a * prev + r_prev
                    h_ref[g, r * n:(r + 1) * n, s * LANES:(s + 1) * LANES] = piece.astype(BF16)
                    if g + 1 < N_PATTERNS:
                        slab_ref[g, s, r * n:(r + 1) * n, :] = piece
        prev = dilation

    def proj(g, col0, width):
        return jnp.dot(h_ref[g], w_ref[:, col0:col0 + width], preferred_element_type=F32)

    def head_norm(z, gain):
        sq = (z * z).astype(BF16)
        half = ones_ref.shape[0]
        parts = [jnp.dot(sq[:, c:c + half], ones_ref[...], preferred_element_type=F32)
                 for c in range(0, z.shape[1], half)]
        ssq = jnp.concatenate(parts, axis=1)
        return z * lax.rsqrt(ssq * (1.0 / HEAD_DIM) + EPS) * gain

    def store(ref, z, kind):
        cols = slice(kind * LANES, (kind + 1) * LANES)
        for j in range(z.shape[1] // LANES):
            piece = z[:, j * LANES:(j + 1) * LANES].astype(BF16)
            if len(ref.shape) == 3:
                ref[j, :, cols] = piece
            else:
                n = ref.shape[2]
                for r in range(ref.shape[1]):
                    ref[j, r, :, cols] = piece[r * n:(r + 1) * n]

    w = PATTERN_WIDTH
    gate_col = w * (1 + 3 * N_PATTERNS)
    for c in range(0, gates_ref.shape[1], w):
        gates_ref[:, c:c + w] = jax.nn.sigmoid(proj(0, gate_col + c, w)).astype(BF16)
    for g, ref in enumerate((qkv0, qkv1, qkv2)):
        for kind, gain_ref in enumerate((qg_ref, kg_ref, None)):
            z = proj(g, w * (1 + kind * N_PATTERNS + g), w)
            store(ref, z if gain_ref is None else head_norm(z, gain_ref[...]), kind)
    u_ref[...] = proj(0, 0, w)


def _in_proj(x, norm_g, w_in, q_gain, k_gain):
    b, seq, d = x.shape
    n_in = w_in.shape[1]
    w = PATTERN_WIDTH
    d_gate = (n_in - w - 3 * N_PATTERNS * w) // 2
    ones_bd = jnp.asarray(np.kron(np.eye(256 // HEAD_DIM), np.ones((HEAD_DIM, HEAD_DIM))), BF16)
    row = lambda width: pl.BlockSpec((None, IN_ROWS, width), lambda i, j: (i, j, 0))

    def pattern_spec(dilation):
        cb = w // LANES
        if dilation == 1:
            spec = pl.BlockSpec((None, cb, IN_ROWS, 3 * LANES), lambda i, j: (i, 0, j, 0))
            return spec, jax.ShapeDtypeStruct((b, cb, seq, 3 * LANES), BF16)
        spec = pl.BlockSpec((None, cb, dilation, IN_ROWS // dilation, 3 * LANES), lambda i, j: (i, 0, 0, j, 0))
        return spec, jax.ShapeDtypeStruct((b, cb, dilation, seq // dilation, 3 * LANES), BF16)

    qkv = [pattern_spec(dilation) for _, dilation in ATTN_PATTERNS]
    out_specs = [row(w)] + [s for s, _ in qkv] + [row(2 * d_gate)]
    out_shape = ([jax.ShapeDtypeStruct((b, seq, w), F32)] + [s for _, s in qkv]
                 + [jax.ShapeDtypeStruct((b, seq, 2 * d_gate), BF16)])
    return pl.pallas_call(
        _in_proj_kernel,
        grid=(b, seq // IN_ROWS),
        in_specs=[row(d), _resident((1, d)), _resident(w_in.shape),
                  _resident((1, w)), _resident((1, w)), _resident(ones_bd.shape)],
        out_specs=out_specs,
        out_shape=out_shape,
        scratch_shapes=[pltpu.VMEM((N_PATTERNS - 1, d // LANES, IN_ROWS, LANES), F32),
                        pltpu.VMEM((N_PATTERNS, IN_ROWS, d), BF16)],
        compiler_params=pltpu.CompilerParams(
            dimension_semantics=("arbitrary", "arbitrary"), vmem_limit_bytes=VMEM_LIMIT_BYTES),
        name="in_proj",
    )(x, norm_g, w_in, q_gain, k_gain, ones_bd)


def _odd_pitch(n):
    p = -(-n // SUBLANES)
    return SUBLANES * (p if p % 2 else p + 1)


def _ssm_stages(first, u_ref, bp_ref, ar_ref, ai_ref, cp_ref, d_ref, y_ref,
                state_ref, bu_ref, xs_ref, yslab_ref, uslab_ref):
    batch, steps, width = u_ref.shape
    u_pitch, y_pitch = _odd_pitch(steps), _odd_pitch(batch)
    n_tiles = width // LANES
    u_tiles = {}

    def prepare():
        @pl.when(first)
        def _():
            state_ref[...] = jnp.zeros_like(state_ref)

        for q in range(n_tiles):
            lanes = slice(q * LANES, (q + 1) * LANES)
            for bi in range(batch):
                uslab_ref[q, bi * u_pitch:bi * u_pitch + steps, :] = u_ref[bi, :, lanes]
            u_tiles[q] = jnp.concatenate(
                [uslab_ref[q, pl.ds(t, batch, stride=u_pitch), :] for t in range(steps)], axis=0)
            ub = u_tiles[q].astype(BF16)
            for gl in range(PAIRS_PER_TILE):
                gp = q * PAIRS_PER_TILE + gl
                bu_ref[q, :, gl * PAIR_COLS:(gl + 1) * PAIR_COLS] = jnp.dot(
                    ub, bp_ref[gp], preferred_element_type=F32)

    def finish(q):
        lanes = slice(q * LANES, (q + 1) * LANES)
        for gl in range(PAIRS_PER_TILE):
            gp = q * PAIRS_PER_TILE + gl
            a_re, a_im = ar_ref[gp], ai_ref[gp]
            xr, xi = state_ref[gp, 0], state_ref[gp, 1]
            c0 = gl * PAIR_COLS
            for t in range(steps):
                rows = slice(t * batch, (t + 1) * batch)
                br = bu_ref[q, rows, c0:c0 + PAIR_STATE]
                bi = bu_ref[q, rows, c0 + PAIR_STATE:c0 + PAIR_COLS]
                xr, xi = a_re * xr - a_im * xi + br, a_re * xi + a_im * xr + bi
                xs_ref[q, rows, c0:c0 + PAIR_STATE] = xr
                xs_ref[q, rows, c0 + PAIR_STATE:c0 + PAIR_COLS] = xi
            state_ref[gp, 0] = xr
            state_ref[gp, 1] = xi

        y = jnp.dot(xs_ref[q].astype(BF16), cp_ref[q], preferred_element_type=F32)
        y = _gelu(y + d_ref[:, lanes] * u_tiles[q])
        for t in range(steps):
            yslab_ref[q, t * y_pitch:t * y_pitch + batch, :] = y[t * batch:(t + 1) * batch]
        for bi in range(batch):
            y_ref[bi, :, lanes] = yslab_ref[q, pl.ds(bi, steps, stride=y_pitch), :].astype(BF16)

    return [prepare] + [functools.partial(finish, q) for q in range(n_tiles)]


def _ssm_scratch(batch, width, n_pairs):
    rows = SSM_STEPS * batch
    n_tiles = width // LANES
    return [pltpu.VMEM((n_pairs, 2, batch, PAIR_STATE), F32),
            pltpu.VMEM((n_tiles, rows, PAIRS_PER_TILE * PAIR_COLS), F32),
            pltpu.VMEM((n_tiles, rows, PAIRS_PER_TILE * PAIR_COLS), F32),
            pltpu.VMEM((n_tiles, SSM_STEPS * _odd_pitch(batch), LANES), F32),
            pltpu.VMEM((n_tiles, batch * _odd_pitch(SSM_STEPS), LANES), F32)]


def _ssm_params(lam_re, lam_im, log_dt, b_re, b_im, c_re, c_im):
    g = lam_re.shape[0]
    dt = jnp.exp(log_dt.astype(F32))[:, None]
    lr = lam_re.astype(F32)
    li = lam_im.astype(F32)
    mag = jnp.exp(lr * dt)
    ar = mag * jnp.cos(li * dt)
    ai = mag * jnp.sin(li * dt)
    den = lr * lr + li * li
    cr = ((ar - 1.0) * lr + ai * li) / den
    ci = (ai * lr - (ar - 1.0) * li) / den
    br = b_re.astype(F32)
    bi = b_im.astype(F32)
    bbar_re = cr[..., None] * br - ci[..., None] * bi
    bbar_im = cr[..., None] * bi + ci[..., None] * br

    n_pairs = g // 2
    n_tiles = n_pairs // PAIRS_PER_TILE
    eye_pt = jnp.eye(PAIRS_PER_TILE, dtype=F32)
    eye_2 = jnp.eye(2, dtype=F32)
    bb = jnp.stack([bbar_re, bbar_im], axis=0).reshape(2, n_tiles, PAIRS_PER_TILE, 2, SSM_STATE, SSM_GROUP)
    bpad = jnp.einsum('aqlgpc,lm,gh->qlmgcahp', bb, eye_pt, eye_2)
    bpad = bpad.reshape(n_pairs, LANES, PAIR_COLS)
    cc = jnp.stack([c_re.astype(F32), -c_im.astype(F32)], axis=0)
    cc = cc.reshape(2, n_tiles, PAIRS_PER_TILE, 2, SSM_GROUP, SSM_STATE)
    cpad = jnp.einsum('aqlgcp,lm,gh->qlagpmhc', cc, eye_pt, eye_2)
    cpad = cpad.reshape(n_tiles, PAIRS_PER_TILE * PAIR_COLS, LANES)
    ar_v = ar.reshape(n_pairs, 1, PAIR_STATE)
    ai_v = ai.reshape(n_pairs, 1, PAIR_STATE)
    return bpad.astype(BF16), ar_v, ai_v, cpad.astype(BF16)


def _alibi_slopes():
    i = np.arange(1, N_ATTN_HEADS + 1, dtype=np.float32)
    s = np.exp2(-8.0 * i / N_ATTN_HEADS).astype(np.float32)
    return s.reshape(HEADS_PER_PATTERN, N_PATTERNS).T


def _attn_bias():
    slopes = _alibi_slopes()
    qi = np.arange(BLOCK)[:, None]
    ki = np.arange(2 * BLOCK)[None, :]
    delta = qi + BLOCK - ki
    heads = LANES // HEAD_DIM
    out = np.empty((N_PATTERNS, HEADS_PER_PATTERN // heads, heads * BLOCK, 2 * BLOCK), np.float32)
    for g, (window, dilation) in enumerate(ATTN_PATTERNS):
        reach = window // dilation
        valid = (delta >= 0) & (delta <= reach)
        dist = (delta * dilation).astype(np.float32)
        for h in range(HEADS_PER_PATTERN):
            rows = slice((h % heads) * BLOCK, (h % heads + 1) * BLOCK)
            out[g, h // heads, rows] = np.where(valid, -slopes[g, h] * dist * LOG2E, -np.inf)
    return out


def _attn_ssm_kernel(qkv0, qkv1, qkv2, bias_ref, u_ref, bp_ref, ar_ref, ai_ref, cp_ref, d_ref, o_ref, y_ref,
                     og0, og1, og2, dg0, dg1, dg2, mg0, mg1, mg2, *ssm_scratch):
    step_index = pl.program_id(0) * pl.num_programs(1) + pl.program_id(1)
    ssm_stages = _ssm_stages(step_index == 0, u_ref, bp_ref, ar_ref, ai_ref, cp_ref, d_ref, y_ref, *ssm_scratch)
    qkv = (qkv0, qkv1, qkv2)
    ogs = (og0, og1, og2)
    dgs = (dg0, dg1, dg2)
    mgs = (mg0, mg1, mg2)
    seq = qkv0.shape[0]
    n_blocks = seq // BLOCK
    q_cols, k_cols, v_cols = (slice(i * LANES, (i + 1) * LANES) for i in range(3))
    lane = lax.broadcasted_iota(jnp.int32, (BLOCK, LANES), 1)
    head0 = lane < HEAD_DIM

    blocks = [(g, bi) for g in range(N_PATTERNS) for bi in range(n_blocks)]

    def is_first(g, bi):
        return bi % (n_blocks // ATTN_PATTERNS[g][1]) == 0

    def split_heads(a):
        lanes = lax.broadcasted_iota(jnp.int32, a.shape, 1)
        keep = (lanes & (LANES - 1)) < HEAD_DIM
        zero = jnp.zeros_like(a)
        return jnp.concatenate([jnp.where(keep, a, zero), jnp.where(keep, zero, a)], axis=0)

    def logits(g, bi):
        ref = qkv[g]
        q = ref[bi * BLOCK:(bi + 1) * BLOCK, q_cols]
        contract_lanes = (((1,), (1,)), ((), ()))
        if is_first(g, bi):
            k = split_heads(ref[bi * BLOCK:(bi + 1) * BLOCK, k_cols])
            s = lax.dot_general(q, k, contract_lanes, preferred_element_type=F32)
            bias = jnp.concatenate([bias_ref[g, :BLOCK, BLOCK:], bias_ref[g, BLOCK:, BLOCK:]], axis=1)
            return s + bias
        k = ref[(bi - 1) * BLOCK:(bi + 1) * BLOCK, k_cols]
        s = lax.dot_general(split_heads(q), k, contract_lanes, preferred_element_type=F32)
        return s + bias_ref[g]

    def softmax_numerator(g, bi, s):
        if is_first(g, bi):
            m0 = jnp.max(s[:, :BLOCK], axis=-1, keepdims=True)
            m1 = jnp.max(s[:, BLOCK:], axis=-1, keepdims=True)
            p = jnp.concatenate([jnp.exp2(s[:, :BLOCK] - m0), jnp.exp2(s[:, BLOCK:] - m1)], axis=1)
            return p.astype(BF16), (m0, m1)
        m = jnp.max(s, axis=-1, keepdims=True)
        return jnp.exp2(s - m).astype(BF16), (m[:BLOCK], m[BLOCK:])

    def weighted_values(g, bi, p, m):
        ref = qkv[g]
        dilation = ATTN_PATTERNS[g][1]
        nb = n_blocks // dilation
        out_start = bi // nb + (bi % nb) * BLOCK * dilation
        dst = pl.ds(out_start, BLOCK) if dilation == 1 else pl.ds(out_start, BLOCK, stride=dilation)
        if is_first(g, bi):
            v = ref[bi * BLOCK:(bi + 1) * BLOCK, v_cols]
            pv = jnp.dot(p, split_heads(jnp.concatenate([v, jnp.ones_like(v)], axis=1)),
                         preferred_element_type=F32)
            ogs[g][dst, :] = pv[:, :LANES]
            dgs[g][dst, :] = pv[:, LANES:]
        else:
            v = ref[(bi - 1) * BLOCK:(bi + 1) * BLOCK, v_cols]
            pv = jnp.dot(p, jnp.concatenate([v, jnp.ones_like(v)], axis=1), preferred_element_type=F32)
            ogs[g][dst, :] = jnp.where(head0, pv[:BLOCK, :LANES], pv[BLOCK:, :LANES])
            dgs[g][dst, :] = jnp.where(head0, pv[:BLOCK, LANES:], pv[BLOCK:, LANES:])
        mgs[g][dst, :] = jnp.where(head0, jnp.broadcast_to(m[0], (BLOCK, LANES)),
                                   jnp.broadcast_to(m[1], (BLOCK, LANES)))

    scores, numer = {}, {}
    groups = [range(i, i + ATTN_GROUP) for i in range(0, len(blocks), ATTN_GROUP)]
    n_steps = len(groups) + 2 * ATTN_LAG
    ssm_at = {(k * n_steps) // len(ssm_stages): stage for k, stage in enumerate(ssm_stages)}
    for step in range(n_steps):
        if step in ssm_at:
            ssm_at[step]()
        if step < len(groups):
            for n in groups[step]:
                scores[n] = logits(*blocks[n])
        if 0 <= step - ATTN_LAG < len(groups):
            for n in groups[step - ATTN_LAG]:
                numer[n] = softmax_numerator(*blocks[n], scores.pop(n))
        if 0 <= step - 2 * ATTN_LAG < len(groups):
            for n in groups[step - 2 * ATTN_LAG]:
                weighted_values(*blocks[n], *numer.pop(n))

    chunk = 2 * BLOCK
    for c in range(0, seq, chunk):
        rows = pl.ds(c, chunk)
        ms = [mg[rows, :] for mg in mgs]
        mx = jnp.maximum(jnp.maximum(ms[0], ms[1]), ms[2])
        es = [jnp.exp2(m - mx) for m in ms]
        tot = es[0] * dgs[0][rows, :] + es[1] * dgs[1][rows, :] + es[2] * dgs[2][rows, :]
        acc = es[0] * ogs[0][rows, :] + es[1] * ogs[1][rows, :] + es[2] * ogs[2][rows, :]
        o_ref[rows, :] = (acc * (1.0 / tot)).astype(BF16)


def _attention_and_ssm(qkv, bias, u, bpad, ar_v, ai_v, cpad, d_skip):
    b, col_blocks, seq, _ = qkv[0].shape
    width = u.shape[2]
    assert col_blocks * b == seq // SSM_STEPS, "one SSM time block per attention grid step"
    window = pl.BlockSpec((None, None, seq, 3 * LANES), lambda j, i: (i, j, 0, 0))
    bias_spec = pl.BlockSpec((N_PATTERNS, None) + bias.shape[2:], lambda j, i: (0, j, 0, 0))
    time_block = pl.BlockSpec((b, SSM_STEPS, width), lambda j, i: (0, j * b + i, 0))
    ssm_weights = (bpad, ar_v, ai_v, cpad, d_skip)
    return pl.pallas_call(
        _attn_ssm_kernel,
        grid=(col_blocks, b),
        in_specs=[window] * N_PATTERNS + [bias_spec, time_block] + [_resident(w.shape) for w in ssm_weights],
        out_specs=[pl.BlockSpec((None, seq, LANES), lambda j, i: (i, 0, j)), time_block],
        out_shape=[jax.ShapeDtypeStruct((b, seq, col_blocks * LANES), BF16),
                   jax.ShapeDtypeStruct((b, seq, width), BF16)],
        scratch_shapes=[pltpu.VMEM((seq, LANES), F32)] * 9 + _ssm_scratch(b, width, bpad.shape[0]),
        compiler_params=pltpu.CompilerParams(
            dimension_semantics=("arbitrary", "arbitrary"), vmem_limit_bytes=VMEM_LIMIT_BYTES),
        name="attn_ssm",
    )(*qkv, bias, u, *ssm_weights)


def _tail_kernel(x_ref, ys_ref, oa_ref, gates_ref, wv_ref, wg_ref, wup_ref, wout_ref,
                 fg_ref, fwg_ref, fwu_ref, cw_ref, cb_ref, fwd_ref, o_ref, carry_ref, gs_ref, h_ref):
    rows, d = x_ref.shape
    halo = SUBLANES

    @pl.when(pl.program_id(1) == 0)
    def _():
        carry_ref[...] = jnp.zeros_like(carry_ref)

    ys = ys_ref[...]
    val = jnp.dot(ys, wv_ref[...], preferred_element_type=F32)
    gate = jnp.dot(ys, wg_ref[...], preferred_element_type=F32)
    y_a = val * jax.nn.sigmoid(gate)
    y_b = jnp.dot(oa_ref[...], wup_ref[...], preferred_element_type=F32)
    merged = gates_ref[:, :d].astype(F32) * y_a + gates_ref[:, d:].astype(F32) * y_b
    x = x_ref[...] + jnp.dot(merged.astype(BF16), wout_ref[...], preferred_element_type=F32)

    ms = jnp.mean(x * x, axis=-1, keepdims=True)
    hf = (x * lax.rsqrt(ms + EPS) * fg_ref[...]).astype(BF16)
    d_ff = fwg_ref.shape[1]
    for c in range(0, d_ff, FFN_COLS):
        cols = slice(c, c + FFN_COLS)
        gate = jnp.dot(hf, fwg_ref[:, cols], preferred_element_type=F32)
        up = jnp.dot(hf, fwu_ref[:, cols], preferred_element_type=F32)
        gs_ref[0:halo, :] = carry_ref[:, cols]
        gs_ref[halo:halo + rows, :] = gate
        carry_ref[:, cols] = gate[rows - halo:rows, :]
        pre = cb_ref[:, cols] + cw_ref[CONV_WIDTH - 1:CONV_WIDTH, cols] * gate
        for tap in range(1, CONV_WIDTH):
            shifted = gs_ref[halo - tap:halo - tap + rows, :]
            pre = pre + cw_ref[CONV_WIDTH - 1 - tap:CONV_WIDTH - tap, cols] * shifted
        h_ref[:, cols] = (_gelu(pre) * up).astype(BF16)
    o_ref[...] = x + jnp.dot(h_ref[...], fwd_ref[...], preferred_element_type=F32)


def _tail(x, ys, oa, gates, wv, wg, wup, wout, norm_g, fwg, fwu, conv_w, conv_b, fwd):
    b, seq, d = x.shape
    d_ff = fwg.shape[1]
    tile = lambda width: pl.BlockSpec((None, TAIL_ROWS, width), lambda i, j: (i, j, 0))
    weights = (wv, wg, wup, wout, norm_g, fwg, fwu, conv_w, conv_b, fwd)
    return pl.pallas_call(
        _tail_kernel,
        grid=(b, seq // TAIL_ROWS),
        in_specs=[tile(d), tile(ys.shape[2]), tile(oa.shape[2]), tile(gates.shape[2])]
                 + [_resident(w.shape) for w in weights],
        out_specs=tile(d),
        out_shape=jax.ShapeDtypeStruct((b, seq, d), F32),
        scratch_shapes=[pltpu.VMEM((SUBLANES, d_ff), F32),
                        pltpu.VMEM((TAIL_ROWS + SUBLANES, FFN_COLS), F32),
                        pltpu.VMEM((TAIL_ROWS, d_ff), BF16)],
        compiler_params=pltpu.CompilerParams(
            dimension_semantics=("arbitrary", "arbitrary"), vmem_limit_bytes=VMEM_LIMIT_BYTES),
        name="tail",
    )(x, ys, oa, gates, *weights)


def _layer(x, norm_mix_g, w_in, lam_re, lam_im, log_dt, b_re, b_im, c_re, c_im, ssm_d,
           glu_w_val, glu_w_gate, q_norm_g, k_norm_g, w_attn_up, w_out, norm_ffn_g,
           ffn_w_gate, ffn_w_up, ffn_conv_w, ffn_conv_b, ffn_w_down):
    b, seq, d = x.shape
    scale = HEAD_DIM ** -0.5 * LOG2E
    q_gain = jnp.tile(q_norm_g.astype(F32) * scale, HEADS_PER_PATTERN)[None, :]
    k_gain = jnp.tile(k_norm_g.astype(F32), HEADS_PER_PATTERN)[None, :]

    outs = _in_proj(x, norm_mix_g.astype(F32)[None, :], w_in.astype(BF16), q_gain, k_gain)
    u, qkv, gates = outs[0], outs[1:1 + N_PATTERNS], outs[1 + N_PATTERNS]

    bpad, ar_v, ai_v, cpad = _ssm_params(lam_re, lam_im, log_dt, b_re, b_im, c_re, c_im)
    attn_in = [a.reshape(b, PATTERN_WIDTH // LANES, seq, 3 * LANES) for a in qkv]
    o_attn, ys = _attention_and_ssm(attn_in, jnp.asarray(_attn_bias()), u, bpad, ar_v, ai_v, cpad,
                                    ssm_d.astype(F32)[None, :])

    out = _tail(x, ys, o_attn, gates, glu_w_val.astype(BF16),
                glu_w_gate.astype(BF16), w_attn_up.astype(BF16), w_out.astype(BF16),
                norm_ffn_g.astype(F32)[None, :], ffn_w_gate.astype(BF16), ffn_w_up.astype(BF16),
                ffn_conv_w.astype(F32), ffn_conv_b.astype(F32)[None, :], ffn_w_down.astype(BF16))
    return out


def kernel(x, norm_mix_g, w_in, ssm_lambda_re, ssm_lambda_im, ssm_log_dt, ssm_b_re, ssm_b_im, ssm_c_re, ssm_c_im, ssm_d, glu_w_val, glu_w_gate, q_norm_g, k_norm_g, w_attn_up, w_out, norm_ffn_g, ffn_w_gate, ffn_w_up, ffn_conv_w, ffn_conv_b, ffn_w_down):
    depth = w_in.shape[0]
    for i in range(depth):
        x = _layer(x, norm_mix_g[i], w_in[i], ssm_lambda_re[i], ssm_lambda_im[i], ssm_log_dt[i],
                   ssm_b_re[i], ssm_b_im[i], ssm_c_re[i], ssm_c_im[i], ssm_d[i], glu_w_val[i],
                   glu_w_gate[i], q_norm_g[i], k_norm_g[i], w_attn_up[i], w_out[i], norm_ffn_g[i],
                   ffn_w_gate[i], ffn_w_up[i], ffn_conv_w[i], ffn_conv_b[i], ffn_w_down[i]).astype(x.dtype)
    return x
```

```python
import functools
import math

import numpy as np
import jax
import jax.numpy as jnp
from jax import lax
from jax.experimental import pallas as pl
from jax.experimental.pallas import tpu as pltpu

F32 = jnp.float32
BF16 = jnp.bfloat16

SSM_GROUP = 16
SSM_STATE = 64
HEAD_DIM = 64
ATTN_PATTERNS = ((128, 1), (512, 4), (2048, 16))
N_PATTERNS = len(ATTN_PATTERNS)
HEADS_PER_PATTERN = 8
N_ATTN_HEADS = N_PATTERNS * HEADS_PER_PATTERN
PATTERN_WIDTH = HEADS_PER_PATTERN * HEAD_DIM
BLOCK = 128
CONV_WIDTH = 3
EPS = 1e-6
LOG2E = math.log2(math.e)

LANES = 128
SUBLANES = 8
VMEM_LIMIT_BYTES = 56 * 1024 * 1024

IN_ROWS = 512
TAIL_ROWS = 512
FFN_COLS = 256
SSM_STEPS = 32
ATTN_GROUP = 1
ATTN_LAG = 2
GROUPS_PER_TILE = LANES // SSM_GROUP
PAIRS_PER_TILE = GROUPS_PER_TILE // 2
PAIR_STATE = 2 * SSM_STATE
PAIR_COLS = 2 * PAIR_STATE


def _resident(shape):
    nd = len(shape)
    return pl.BlockSpec(shape, lambda *_: (0,) * nd, pipeline_mode=pl.Buffered(1))


def _gelu(z):
    return 0.5 * z * (1.0 + lax.erf(z * (1.0 / math.sqrt(2.0))))


def _in_proj_kernel(x_ref, g_ref, w_ref, qg_ref, kg_ref,
                    u_ref, qkv0, qkv1, qkv2, gates_ref, slab_ref, h_ref):
    rows, d = x_ref.shape
    n_slabs = d // LANES
    x = x_ref[...]
    ms = jnp.mean(x * x, axis=-1, keepdims=True)
    hn = x * lax.rsqrt(ms + EPS) * g_ref[...]
    h_ref[0] = hn.astype(BF16)
    for s in range(n_slabs):
        slab_ref[0, s] = hn[:, s * LANES:(s + 1) * LANES]

    prev = 1
    for g in range(1, N_PATTERNS):
        dilation = ATTN_PATTERNS[g][1]
        factor = dilation // prev
        n_prev, n = rows // prev, rows // dilation
        for s in range(n_slabs):
            for r_prev in range(prev):
                for a in range(factor):
                    piece = slab_ref[g - 1, s, pl.ds(r_prev * n_prev + a, n, stride=factor), :]
                    r = a * prev + r_prev
                    h_ref[g, r * n:(r + 1) * n, s * LANES:(s + 1) * LANES] = piece.astype(BF16)
                    if g + 1 < N_PATTERNS:
                        slab_ref[g, s, r * n:(r + 1) * n, :] = piece
        prev = dilation

    def proj(g, col0, width):
        return jnp.dot(h_ref[g], w_ref[:, col0:col0 + width], preferred_element_type=F32)

    def head_norm(z, gain):
        lane = lax.broadcasted_iota(jnp.int32, (z.shape[0], LANES), 1)
        first = lane < HEAD_DIM
        parts = []
        for c in range(0, z.shape[1], LANES):
            sq = z[:, c:c + LANES] * z[:, c:c + LANES]
            s0 = jnp.sum(jnp.where(first, sq, 0.0), axis=-1, keepdims=True)
            s1 = jnp.sum(jnp.where(first, 0.0, sq), axis=-1, keepdims=True)
            parts.append(jnp.where(first, s0, s1))
        ssq = jnp.concatenate(parts, axis=1)
        return z * lax.rsqrt(ssq * (1.0 / HEAD_DIM) + EPS) * gain

    def store(ref, z, kind):
        cols = slice(kind * LANES, (kind + 1) * LANES)
        for j in range(z.shape[1] // LANES):
            piece = z[:, j * LANES:(j + 1) * LANES].astype(BF16)
            if len(ref.shape) == 3:
                ref[j, :, cols] = piece
            else:
                n = ref.shape[2]
                for r in range(ref.shape[1]):
                    ref[j, r, :, cols] = piece[r * n:(r + 1) * n]

    w = PATTERN_WIDTH
    gate_col = w * (1 + 3 * N_PATTERNS)
    for c in range(0, gates_ref.shape[1], w):
        gates_ref[:, c:c + w] = jax.nn.sigmoid(proj(0, gate_col + c, w)).astype(BF16)
    for g, ref in enumerate((qkv0, qkv1, qkv2)):
        for kind, gain_ref in enumerate((qg_ref, kg_ref, None)):
            z = proj(g, w * (1 + kind * N_PATTERNS + g), w)
            store(ref, z if gain_ref is None else head_norm(z, gain_ref[...]), kind)
    u_ref[...] = proj(0, 0, w)


def _in_proj(x, norm_g, w_in, q_gain, k_gain):
    b, seq, d = x.shape
    n_in = w_in.shape[1]
    w = PATTERN_WIDTH
    d_gate = (n_in - w - 3 * N_PATTERNS * w) // 2
    row = lambda width: pl.BlockSpec((None, IN_ROWS, width), lambda i, j: (i, j, 0))

    def pattern_spec(dilation):
        cb = w // LANES
        if dilation == 1:
            spec = pl.BlockSpec((None, cb, IN_ROWS, 3 * LANES), lambda i, j: (i, 0, j, 0))
            return spec, jax.ShapeDtypeStruct((b, cb, seq, 3 * LANES), BF16)
        spec = pl.BlockSpec((None, cb, dilation, IN_ROWS // dilation, 3 * LANES), lambda i, j: (i, 0, 0, j, 0))
        return spec, jax.ShapeDtypeStruct((b, cb, dilation, seq // dilation, 3 * LANES), BF16)

    qkv = [pattern_spec(dilation) for _, dilation in ATTN_PATTERNS]
    out_specs = [row(w)] + [s for s, _ in qkv] + [row(2 * d_gate)]
    out_shape = ([jax.ShapeDtypeStruct((b, seq, w), F32)] + [s for _, s in qkv]
                 + [jax.ShapeDtypeStruct((b, seq, 2 * d_gate), BF16)])
    return pl.pallas_call(
        _in_proj_kernel,
        grid=(b, seq // IN_ROWS),
        in_specs=[row(d), _resident((1, d)), _resident(w_in.shape),
                  _resident((1, w)), _resident((1, w))],
        out_specs=out_specs,
        out_shape=out_shape,
        scratch_shapes=[pltpu.VMEM((N_PATTERNS - 1, d // LANES, IN_ROWS, LANES), F32),
                        pltpu.VMEM((N_PATTERNS, IN_ROWS, d), BF16)],
        compiler_params=pltpu.CompilerParams(
            dimension_semantics=("arbitrary", "arbitrary"), vmem_limit_bytes=VMEM_LIMIT_BYTES),
        name="in_proj",
    )(x, norm_g, w_in, q_gain, k_gain)


def _odd_pitch(n):
    p = -(-n // SUBLANES)
    return SUBLANES * (p if p % 2 else p + 1)


def _ssm_stages(first, u_ref, bp_ref, ar_ref, ai_ref, cp_ref, d_ref, y_ref,
                state_ref, bu_ref, xs_ref, yslab_ref, uslab_ref):
    batch, steps, width = u_ref.shape
    u_pitch, y_pitch = _odd_pitch(steps), _odd_pitch(batch)
    n_tiles = width // LANES
    u_tiles = {}

    def prepare():
        @pl.when(first)
        def _():
            state_ref[...] = jnp.zeros_like(state_ref)

        for q in range(n_tiles):
            lanes = slice(q * LANES, (q + 1) * LANES)
            for bi in range(batch):
                uslab_ref[q, bi * u_pitch:bi * u_pitch + steps, :] = u_ref[bi, :, lanes]
            u_tiles[q] = jnp.concatenate(
                [uslab_ref[q, pl.ds(t, batch, stride=u_pitch), :] for t in range(steps)], axis=0)
            ub = u_tiles[q].astype(BF16)
            for gl in range(PAIRS_PER_TILE):
                gp = q * PAIRS_PER_TILE + gl
                bu_ref[q, :, gl * PAIR_COLS:(gl + 1) * PAIR_COLS] = jnp.dot(
                    ub, bp_ref[gp], preferred_element_type=F32)

    def finish(q):
        lanes = slice(q * LANES, (q + 1) * LANES)
        for gl in range(PAIRS_PER_TILE):
            gp = q * PAIRS_PER_TILE + gl
            a_re, a_im = ar_ref[gp], ai_ref[gp]
            xr, xi = state_ref[gp, 0], state_ref[gp, 1]
            c0 = gl * PAIR_COLS
            for t in range(steps):
                rows = slice(t * batch, (t + 1) * batch)
                br = bu_ref[q, rows, c0:c0 + PAIR_STATE]
                bi = bu_ref[q, rows, c0 + PAIR_STATE:c0 + PAIR_COLS]
                xr, xi = a_re * xr - a_im * xi + br, a_re * xi + a_im * xr + bi
                xs_ref[q, rows, c0:c0 + PAIR_STATE] = xr
                xs_ref[q, rows, c0 + PAIR_STATE:c0 + PAIR_COLS] = xi
            state_ref[gp, 0] = xr
            state_ref[gp, 1] = xi

        y = jnp.dot(xs_ref[q].astype(BF16), cp_ref[q], preferred_element_type=F32)
        y = _gelu(y + d_ref[:, lanes] * u_tiles[q])
        for t in range(steps):
            yslab_ref[q, t * y_pitch:t * y_pitch + batch, :] = y[t * batch:(t + 1) * batch]
        for bi in range(batch):
            y_ref[bi, :, lanes] = yslab_ref[q, pl.ds(bi, steps, stride=y_pitch), :].astype(BF16)

    return [prepare] + [functools.partial(finish, q) for q in range(n_tiles)]


def _ssm_scratch(batch, width, n_pairs):
    rows = SSM_STEPS * batch
    n_tiles = width // LANES
    return [pltpu.VMEM((n_pairs, 2, batch, PAIR_STATE), F32),
            pltpu.VMEM((n_tiles, rows, PAIRS_PER_TILE * PAIR_COLS), F32),
            pltpu.VMEM((n_tiles, rows, PAIRS_PER_TILE * PAIR_COLS), F32),
            pltpu.VMEM((n_tiles, SSM_STEPS * _odd_pitch(batch), LANES), F32),
            pltpu.VMEM((n_tiles, batch * _odd_pitch(SSM_STEPS), LANES), F32)]


def _ssm_params(lam_re, lam_im, log_dt, b_re, b_im, c_re, c_im):
    g = lam_re.shape[0]
    dt = jnp.exp(log_dt.astype(F32))[:, None]
    lr = lam_re.astype(F32)
    li = lam_im.astype(F32)
    mag = jnp.exp(lr * dt)
    ar = mag * jnp.cos(li * dt)
    ai = mag * jnp.sin(li * dt)
    den = lr * lr + li * li
    cr = ((ar - 1.0) * lr + ai * li) / den
    ci = (ai * lr - (ar - 1.0) * li) / den
    br = b_re.astype(F32)
    bi = b_im.astype(F32)
    bbar_re = cr[..., None] * br - ci[..., None] * bi
    bbar_im = cr[..., None] * bi + ci[..., None] * br

    n_pairs = g // 2
    n_tiles = n_pairs // PAIRS_PER_TILE
    eye_pt = jnp.eye(PAIRS_PER_TILE, dtype=F32)
    eye_2 = jnp.eye(2, dtype=F32)
    bb = jnp.stack([bbar_re, bbar_im], axis=0).reshape(2, n_tiles, PAIRS_PER_TILE, 2, SSM_STATE, SSM_GROUP)
    bpad = jnp.einsum('aqlgpc,lm,gh->qlmgcahp', bb, eye_pt, eye_2)
    bpad = bpad.reshape(n_pairs, LANES, PAIR_COLS)
    cc = jnp.stack([c_re.astype(F32), -c_im.astype(F32)], axis=0)
    cc = cc.reshape(2, n_tiles, PAIRS_PER_TILE, 2, SSM_GROUP, SSM_STATE)
    cpad = jnp.einsum('aqlgcp,lm,gh->qlagpmhc', cc, eye_pt, eye_2)
    cpad = cpad.reshape(n_tiles, PAIRS_PER_TILE * PAIR_COLS, LANES)
    ar_v = ar.reshape(n_pairs, 1, PAIR_STATE)
    ai_v = ai.reshape(n_pairs, 1, PAIR_STATE)
    return bpad.astype(BF16), ar_v, ai_v, cpad.astype(BF16)


def _alibi_slopes():
    i = np.arange(1, N_ATTN_HEADS + 1, dtype=np.float32)
    s = np.exp2(-8.0 * i / N_ATTN_HEADS).astype(np.float32)
    return s.reshape(HEADS_PER_PATTERN, N_PATTERNS).T


def _attn_bias():
    slopes = _alibi_slopes()
    qi = np.arange(BLOCK)[:, None]
    ki = np.arange(2 * BLOCK)[None, :]
    delta = qi + BLOCK - ki
    heads = LANES // HEAD_DIM
    out = np.empty((N_PATTERNS, HEADS_PER_PATTERN // heads, heads * BLOCK, 2 * BLOCK), np.float32)
    for g, (window, dilation) in enumerate(ATTN_PATTERNS):
        reach = window // dilation
        valid = (delta >= 0) & (delta <= reach)
        dist = (delta * dilation).astype(np.float32)
        for h in range(HEADS_PER_PATTERN):
            rows = slice((h % heads) * BLOCK, (h % heads + 1) * BLOCK)
            out[g, h // heads, rows] = np.where(valid, -slopes[g, h] * dist * LOG2E, -np.inf)
    return out


def _attn_ssm_kernel(qkv0, qkv1, qkv2, bias_ref, u_ref, bp_ref, ar_ref, ai_ref, cp_ref, d_ref, o_ref, y_ref,
                     og0, og1, og2, dg0, dg1, dg2, mg0, mg1, mg2, *ssm_scratch):
    step_index = pl.program_id(0) * pl.num_programs(1) + pl.program_id(1)
    ssm_stages = _ssm_stages(step_index == 0, u_ref, bp_ref, ar_ref, ai_ref, cp_ref, d_ref, y_ref, *ssm_scratch)
    qkv = (qkv0, qkv1, qkv2)
    ogs = (og0, og1, og2)
    dgs = (dg0, dg1, dg2)
    mgs = (mg0, mg1, mg2)
    seq = qkv0.shape[0]
    n_blocks = seq // BLOCK
    q_cols, k_cols, v_cols = (slice(i * LANES, (i + 1) * LANES) for i in range(3))
    lane = lax.broadcasted_iota(jnp.int32, (BLOCK, LANES), 1)
    head0 = lane < HEAD_DIM

    blocks = [(g, bi) for g in range(N_PATTERNS) for bi in range(n_blocks)]

    def is_first(g, bi):
        return bi % (n_blocks // ATTN_PATTERNS[g][1]) == 0

    def split_heads(a):
        lanes = lax.broadcasted_iota(jnp.int32, a.shape, 1)
        keep = (lanes & (LANES - 1)) < HEAD_DIM
        zero = jnp.zeros_like(a)
        return jnp.concatenate([jnp.where(keep, a, zero), jnp.where(keep, zero, a)], axis=0)

    def logits(g, bi):
        ref = qkv[g]
        q = ref[bi * BLOCK:(bi + 1) * BLOCK, q_cols]
        contract_lanes = (((1,), (1,)), ((), ()))
        if is_first(g, bi):
            k = split_heads(ref[bi * BLOCK:(bi + 1) * BLOCK, k_cols])
            s = lax.dot_general(q, k, contract_lanes, preferred_element_type=F32)
            bias = jnp.concatenate([bias_ref[g, :BLOCK, BLOCK:], bias_ref[g, BLOCK:, BLOCK:]], axis=1)
            return s + bias
        k = ref[(bi - 1) * BLOCK:(bi + 1) * BLOCK, k_cols]
        s = lax.dot_general(split_heads(q), k, contract_lanes, preferred_element_type=F32)
        return s + bias_ref[g]

    def softmax_numerator(g, bi, s):
        if is_first(g, bi):
            m0 = jnp.max(s[:, :BLOCK], axis=-1, keepdims=True)
            m1 = jnp.max(s[:, BLOCK:], axis=-1, keepdims=True)
            p = jnp.concatenate([jnp.exp2(s[:, :BLOCK] - m0), jnp.exp2(s[:, BLOCK:] - m1)], axis=1)
            return p.astype(BF16), (m0, m1)
        m = jnp.max(s, axis=-1, keepdims=True)
        return jnp.exp2(s - m).astype(BF16), (m[:BLOCK], m[BLOCK:])

    def weighted_values(g, bi, p, m):
        ref = qkv[g]
        dilation = ATTN_PATTERNS[g][1]
        nb = n_blocks // dilation
        out_start = bi // nb + (bi % nb) * BLOCK * dilation
        dst = pl.ds(out_start, BLOCK) if dilation == 1 else pl.ds(out_start, BLOCK, stride=dilation)
        if is_first(g, bi):
            v = ref[bi * BLOCK:(bi + 1) * BLOCK, v_cols]
            pv = jnp.dot(p, split_heads(jnp.concatenate([v, jnp.ones_like(v)], axis=1)),
                         preferred_element_type=F32)
            ogs[g][dst, :] = pv[:, :LANES]
            dgs[g][dst, :] = pv[:, LANES:]
        else:
            v = ref[(bi - 1) * BLOCK:(bi + 1) * BLOCK, v_cols]
            pv = jnp.dot(p, jnp.concatenate([v, jnp.ones_like(v)], axis=1), preferred_element_type=F32)
            ogs[g][dst, :] = jnp.where(head0, pv[:BLOCK, :LANES], pv[BLOCK:, :LANES])
            dgs[g][dst, :] = jnp.where(head0, pv[:BLOCK, LANES:], pv[BLOCK:, LANES:])
        mgs[g][dst, :] = jnp.where(head0, jnp.broadcast_to(m[0], (BLOCK, LANES)),
                                   jnp.broadcast_to(m[1], (BLOCK, LANES)))

    scores, numer = {}, {}
    groups = [range(i, i + ATTN_GROUP) for i in range(0, len(blocks), ATTN_GROUP)]
    n_steps = len(groups) + 2 * ATTN_LAG
    ssm_at = {(k * n_steps) // len(ssm_stages): stage for k, stage in enumerate(ssm_stages)}
    for step in range(n_steps):
        if step in ssm_at:
            ssm_at[step]()
        if step < len(groups):
            for n in groups[step]:
                scores[n] = logits(*blocks[n])
        if 0 <= step - ATTN_LAG < len(groups):
            for n in groups[step - ATTN_LAG]:
                numer[n] = softmax_numerator(*blocks[n], scores.pop(n))
        if 0 <= step - 2 * ATTN_LAG < len(groups):
            for n in groups[step - 2 * ATTN_LAG]:
                weighted_values(*blocks[n], *numer.pop(n))

    chunk = 2 * BLOCK
    for c in range(0, seq, chunk):
        rows = pl.ds(c, chunk)
        ms = [mg[rows, :] for mg in mgs]
        mx = jnp.maximum(jnp.maximum(ms[0], ms[1]), ms[2])
        es = [jnp.exp2(m - mx) for m in ms]
        tot = es[0] * dgs[0][rows, :] + es[1] * dgs[1][rows, :] + es[2] * dgs[2][rows, :]
        acc = es[0] * ogs[0][rows, :] + es[1] * ogs[1][rows, :] + es[2] * ogs[2][rows, :]
        o_ref[rows, :] = (acc * (1.0 / tot)).astype(BF16)


def _attention_and_ssm(qkv, bias, u, bpad, ar_v, ai_v, cpad, d_skip):
    b, col_blocks, seq, _ = qkv[0].shape
    width = u.shape[2]
    assert col_blocks * b == seq // SSM_STEPS, "one SSM time block per attention grid step"
    window = pl.BlockSpec((None, None, seq, 3 * LANES), lambda j, i: (i, j, 0, 0))
    bias_spec = pl.BlockSpec((N_PATTERNS, None) + bias.shape[2:], lambda j, i: (0, j, 0, 0))
    time_block = pl.BlockSpec((b, SSM_STEPS, width), lambda j, i: (0, j * b + i, 0))
    ssm_weights = (bpad, ar_v, ai_v, cpad, d_skip)
    return pl.pallas_call(
        _attn_ssm_kernel,
        grid=(col_blocks, b),
        in_specs=[window] * N_PATTERNS + [bias_spec, time_block] + [_resident(w.shape) for w in ssm_weights],
        out_specs=[pl.BlockSpec((None, seq, LANES), lambda j, i: (i, 0, j)), time_block],
        out_shape=[jax.ShapeDtypeStruct((b, seq, col_blocks * LANES), BF16),
                   jax.ShapeDtypeStruct((b, seq, width), BF16)],
        scratch_shapes=[pltpu.VMEM((seq, LANES), F32)] * 9 + _ssm_scratch(b, width, bpad.shape[0]),
        compiler_params=pltpu.CompilerParams(
            dimension_semantics=("arbitrary", "arbitrary"), vmem_limit_bytes=VMEM_LIMIT_BYTES),
        name="attn_ssm",
    )(*qkv, bias, u, *ssm_weights)


def _tail_kernel(x_ref, ys_ref, oa_ref, gates_ref, wv_ref, wg_ref, wup_ref, wout_ref,
                 fg_ref, fwg_ref, fwu_ref, cw_ref, cb_ref, fwd_ref, o_ref, carry_ref, gs_ref, h_ref):
    rows, d = x_ref.shape
    halo = SUBLANES

    @pl.when(pl.program_id(1) == 0)
    def _():
        carry_ref[...] = jnp.zeros_like(carry_ref)

    ys = ys_ref[...]
    val = jnp.dot(ys, wv_ref[...], preferred_element_type=F32)
    gate = jnp.dot(ys, wg_ref[...], preferred_element_type=F32)
    y_a = val * jax.nn.sigmoid(gate)
    y_b = jnp.dot(oa_ref[...], wup_ref[...], preferred_element_type=F32)
    merged = gates_ref[:, :d].astype(F32) * y_a + gates_ref[:, d:].astype(F32) * y_b
    x = x_ref[...] + jnp.dot(merged.astype(BF16), wout_ref[...], preferred_element_type=F32)

    ms = jnp.mean(x * x, axis=-1, keepdims=True)
    hf = (x * lax.rsqrt(ms + EPS) * fg_ref[...]).astype(BF16)
    d_ff = fwg_ref.shape[1]
    for c in range(0, d_ff, FFN_COLS):
        cols = slice(c, c + FFN_COLS)
        gate = jnp.dot(hf, fwg_ref[:, cols], preferred_element_type=F32)
        up = jnp.dot(hf, fwu_ref[:, cols], preferred_element_type=F32)
        gs_ref[0:halo, :] = carry_ref[:, cols]
        gs_ref[halo:halo + rows, :] = gate
        carry_ref[:, cols] = gate[rows - halo:rows, :]
        pre = cb_ref[:, cols] + cw_ref[CONV_WIDTH - 1:CONV_WIDTH, cols] * gate
        for tap in range(1, CONV_WIDTH):
            shifted = gs_ref[halo - tap:halo - tap + rows, :]
            pre = pre + cw_ref[CONV_WIDTH - 1 - tap:CONV_WIDTH - tap, cols] * shifted
        h_ref[:, cols] = (_gelu(pre) * up).astype(BF16)
    o_ref[...] = x + jnp.dot(h_ref[...], fwd_ref[...], preferred_element_type=F32)


def _tail(x, ys, oa, gates, wv, wg, wup, wout, norm_g, fwg, fwu, conv_w, conv_b, fwd):
    b, seq, d = x.shape
    d_ff = fwg.shape[1]
    tile = lambda width: pl.BlockSpec((None, TAIL_ROWS, width), lambda i, j: (i, j, 0))
    weights = (wv, wg, wup, wout, norm_g, fwg, fwu, conv_w, conv_b, fwd)
    return pl.pallas_call(
        _tail_kernel,
        grid=(b, seq // TAIL_ROWS),
        in_specs=[tile(d), tile(ys.shape[2]), tile(oa.shape[2]), tile(gates.shape[2])]
                 + [_resident(w.shape) for w in weights],
        out_specs=tile(d),
        out_shape=jax.ShapeDtypeStruct((b, seq, d), F32),
        scratch_shapes=[pltpu.VMEM((SUBLANES, d_ff), F32),
                        pltpu.VMEM((TAIL_ROWS + SUBLANES, FFN_COLS), F32),
                        pltpu.VMEM((TAIL_ROWS, d_ff), BF16)],
        compiler_params=pltpu.CompilerParams(
            dimension_semantics=("arbitrary", "arbitrary"), vmem_limit_bytes=VMEM_LIMIT_BYTES),
        name="tail",
    )(x, ys, oa, gates, *weights)


def _layer(x, norm_mix_g, w_in, lam_re, lam_im, log_dt, b_re, b_im, c_re, c_im, ssm_d,
           glu_w_val, glu_w_gate, q_norm_g, k_norm_g, w_attn_up, w_out, norm_ffn_g,
           ffn_w_gate, ffn_w_up, ffn_conv_w, ffn_conv_b, ffn_w_down):
    b, seq, d = x.shape
    scale = HEAD_DIM ** -0.5 * LOG2E
    q_gain = jnp.tile(q_norm_g.astype(F32) * scale, HEADS_PER_PATTERN)[None, :]
    k_gain = jnp.tile(k_norm_g.astype(F32), HEADS_PER_PATTERN)[None, :]

    outs = _in_proj(x, norm_mix_g.astype(F32)[None, :], w_in.astype(BF16), q_gain, k_gain)
    u, qkv, gates = outs[0], outs[1:1 + N_PATTERNS], outs[1 + N_PATTERNS]

    bpad, ar_v, ai_v, cpad = _ssm_params(lam_re, lam_im, log_dt, b_re, b_im, c_re, c_im)
    attn_in = [a.reshape(b, PATTERN_WIDTH // LANES, seq, 3 * LANES) for a in qkv]
    o_attn, ys = _attention_and_ssm(attn_in, jnp.asarray(_attn_bias()), u, bpad, ar_v, ai_v, cpad,
                                    ssm_d.astype(F32)[None, :])

    out = _tail(x, ys, o_attn, gates, glu_w_val.astype(BF16),
                glu_w_gate.astype(BF16), w_attn_up.astype(BF16), w_out.astype(BF16),
                norm_ffn_g.astype(F32)[None, :], ffn_w_gate.astype(BF16), ffn_w_up.astype(BF16),
                ffn_conv_w.astype(F32), ffn_conv_b.astype(F32)[None, :], ffn_w_down.astype(BF16))
    return out


def kernel(x, norm_mix_g, w_in, ssm_lambda_re, ssm_lambda_im, ssm_log_dt, ssm_b_re, ssm_b_im, ssm_c_re, ssm_c_im, ssm_d, glu_w_val, glu_w_gate, q_norm_g, k_norm_g, w_attn_up, w_out, norm_ffn_g, ffn_w_gate, ffn_w_up, ffn_conv_w, ffn_conv_b, ffn_w_down):
    depth = w_in.shape[0]
    for i in range(depth):
        x = _layer(x, norm_mix_g[i], w_in[i], ssm_lambda_re[i], ssm_lambda_im[i], ssm_log_dt[i],
                   ssm_b_re[i], ssm_b_im[i], ssm_c_re[i], ssm_c_im[i], ssm_d[i], glu_w_val[i],
                   glu_w_gate[i], q_norm_g[i], k_norm_g[i], w_attn_up[i], w_out[i], norm_ffn_g[i],
                   ffn_w_gate[i], ffn_w_up[i], ffn_conv_w[i], ffn_conv_b[i], ffn_w_down[i]).astype(x.dtype)
    return x
```

```python
import functools
import math

import numpy as np
import jax
import jax.numpy as jnp
from jax import lax
from jax.experimental import pallas as pl
from jax.experimental.pallas import tpu as pltpu

F32 = jnp.float32
BF16 = jnp.bfloat16

SSM_GROUP = 16
SSM_STATE = 64
HEAD_DIM = 64
ATTN_PATTERNS = ((128, 1), (512, 4), (2048, 16))
N_PATTERNS = len(ATTN_PATTERNS)
HEADS_PER_PATTERN = 8
N_ATTN_HEADS = N_PATTERNS * HEADS_PER_PATTERN
PATTERN_WIDTH = HEADS_PER_PATTERN * HEAD_DIM
BLOCK = 128
CONV_WIDTH = 3
EPS = 1e-6
LOG2E = math.log2(math.e)

LANES = 128
SUBLANES = 8
VMEM_LIMIT_BYTES = 56 * 1024 * 1024

IN_ROWS = 512
TAIL_ROWS = 512
FFN_COLS = 256
SSM_STEPS = 32
ATTN_LAG = 2
GROUPS_PER_TILE = LANES // SSM_GROUP
PAIRS_PER_TILE = GROUPS_PER_TILE // 2
PAIR_STATE = 2 * SSM_STATE
PAIR_COLS = 2 * PAIR_STATE


def _resident(shape):
    nd = len(shape)
    return pl.BlockSpec(shape, lambda *_: (0,) * nd, pipeline_mode=pl.Buffered(1))


def _gelu(z):
    return 0.5 * z * (1.0 + lax.erf(z * (1.0 / math.sqrt(2.0))))


def _in_proj_kernel(x_ref, g_ref, w_ref, qg_ref, kg_ref,
                    u_ref, qkv0, qkv1, qkv2, gates_ref, slab_ref, h_ref):
    rows, d = x_ref.shape
    n_slabs = d // LANES
    x = x_ref[...]
    ms = jnp.mean(x * x, axis=-1, keepdims=True)
    hn = x * lax.rsqrt(ms + EPS) * g_ref[...]
    h_ref[0] = hn.astype(BF16)
    for s in range(n_slabs):
        slab_ref[0, s] = hn[:, s * LANES:(s + 1) * LANES]

    prev = 1
    for g in range(1, N_PATTERNS):
        dilation = ATTN_PATTERNS[g][1]
        factor = dilation // prev
        n_prev, n = rows // prev, rows // dilation
        for s in range(n_slabs):
            for r_prev in range(prev):
                for a in range(factor):
                    piece = slab_ref[g - 1, s, pl.ds(r_prev * n_prev + a, n, stride=factor), :]
                    r = a * prev + r_prev
                    h_ref[g, r * n:(r + 1) * n, s * LANES:(s + 1) * LANES] = piece.astype(BF16)
                    if g + 1 < N_PATTERNS:
                        slab_ref[g, s, r * n:(r + 1) * n, :] = piece
        prev = dilation

    def proj(g, col0, width):
        return jnp.dot(h_ref[g], w_ref[:, col0:col0 + width], preferred_element_type=F32)

    def head_norm(z, gain):
        lane = lax.broadcasted_iota(jnp.int32, (z.shape[0], LANES), 1)
        first = lane < HEAD_DIM
        parts = []
        for c in range(0, z.shape[1], LANES):
            sq = z[:, c:c + LANES] * z[:, c:c + LANES]
            s0 = jnp.sum(jnp.where(first, sq, 0.0), axis=-1, keepdims=True)
            s1 = jnp.sum(jnp.where(first, 0.0, sq), axis=-1, keepdims=True)
            parts.append(jnp.where(first, s0, s1))
        ssq = jnp.concatenate(parts, axis=1)
        return z * lax.rsqrt(ssq * (1.0 / HEAD_DIM) + EPS) * gain

    def store(ref, z, kind):
        cols = slice(kind * LANES, (kind + 1) * LANES)
        for j in range(z.shape[1] // LANES):
            piece = z[:, j * LANES:(j + 1) * LANES].astype(BF16)
            if len(ref.shape) == 3:
                ref[j, :, cols] = piece
            else:
                n = ref.shape[2]
                for r in range(ref.shape[1]):
                    ref[j, r, :, cols] = piece[r * n:(r + 1) * n]

    w = PATTERN_WIDTH
    gate_col = w * (1 + 3 * N_PATTERNS)
    for c in range(0, gates_ref.shape[1], w):
        gates_ref[:, c:c + w] = jax.nn.sigmoid(proj(0, gate_col + c, w)).astype(BF16)
    for g, ref in enumerate((qkv0, qkv1, qkv2)):
        for kind, gain_ref in enumerate((qg_ref, kg_ref, None)):
            z = proj(g, w * (1 + kind * N_PATTERNS + g), w)
            store(ref, z if gain_ref is None else head_norm(z, gain_ref[...]), kind)
    u_ref[...] = proj(0, 0, w)


def _in_proj(x, norm_g, w_in, q_gain, k_gain):
    b, seq, d = x.shape
    n_in = w_in.shape[1]
    w = PATTERN_WIDTH
    d_gate = (n_in - w - 3 * N_PATTERNS * w) // 2
    row = lambda width: pl.BlockSpec((None, IN_ROWS, width), lambda i, j: (i, j, 0))

    def pattern_spec(dilation):
        cb = w // LANES
        if dilation == 1:
            spec = pl.BlockSpec((None, cb, IN_ROWS, 3 * LANES), lambda i, j: (i, 0, j, 0))
            return spec, jax.ShapeDtypeStruct((b, cb, seq, 3 * LANES), BF16)
        spec = pl.BlockSpec((None, cb, dilation, IN_ROWS // dilation, 3 * LANES), lambda i, j: (i, 0, 0, j, 0))
        return spec, jax.ShapeDtypeStruct((b, cb, dilation, seq // dilation, 3 * LANES), BF16)

    qkv = [pattern_spec(dilation) for _, dilation in ATTN_PATTERNS]
    out_specs = [row(w)] + [s for s, _ in qkv] + [row(2 * d_gate)]
    out_shape = ([jax.ShapeDtypeStruct((b, seq, w), F32)] + [s for _, s in qkv]
                 + [jax.ShapeDtypeStruct((b, seq, 2 * d_gate), BF16)])
    return pl.pallas_call(
        _in_proj_kernel,
        grid=(b, seq // IN_ROWS),
        in_specs=[row(d), _resident((1, d)), _resident(w_in.shape),
                  _resident((1, w)), _resident((1, w))],
        out_specs=out_specs,
        out_shape=out_shape,
        scratch_shapes=[pltpu.VMEM((N_PATTERNS - 1, d // LANES, IN_ROWS, LANES), F32),
                        pltpu.VMEM((N_PATTERNS, IN_ROWS, d), BF16)],
        compiler_params=pltpu.CompilerParams(
            dimension_semantics=("arbitrary", "arbitrary"), vmem_limit_bytes=VMEM_LIMIT_BYTES),
        name="in_proj",
    )(x, norm_g, w_in, q_gain, k_gain)


def _odd_pitch(n):
    p = -(-n // SUBLANES)
    return SUBLANES * (p if p % 2 else p + 1)


def _ssm_stages(first, u_ref, bp_ref, ar_ref, ai_ref, cp_ref, d_ref, y_ref,
                state_ref, bu_ref, xs_ref, yslab_ref, uslab_ref):
    batch, steps, width = u_ref.shape
    u_pitch, y_pitch = _odd_pitch(steps), _odd_pitch(batch)
    n_tiles = width // LANES
    u_tiles = {}

    def prepare():
        @pl.when(first)
        def _():
            state_ref[...] = jnp.zeros_like(state_ref)

        for q in range(n_tiles):
            lanes = slice(q * LANES, (q + 1) * LANES)
            for bi in range(batch):
                uslab_ref[q, bi * u_pitch:bi * u_pitch + steps, :] = u_ref[bi, :, lanes]
            u_tiles[q] = jnp.concatenate(
                [uslab_ref[q, pl.ds(t, batch, stride=u_pitch), :] for t in range(steps)], axis=0)
            ub = u_tiles[q].astype(BF16)
            for gl in range(PAIRS_PER_TILE):
                gp = q * PAIRS_PER_TILE + gl
                bu_ref[q, :, gl * PAIR_COLS:(gl + 1) * PAIR_COLS] = jnp.dot(
                    ub, bp_ref[gp], preferred_element_type=F32)

    def finish(q):
        lanes = slice(q * LANES, (q + 1) * LANES)
        for gl in range(PAIRS_PER_TILE):
            gp = q * PAIRS_PER_TILE + gl
            a_re, a_im = ar_ref[gp], ai_ref[gp]
            xr, xi = state_ref[gp, 0], state_ref[gp, 1]
            c0 = gl * PAIR_COLS
            for t in range(steps):
                rows = slice(t * batch, (t + 1) * batch)
                br = bu_ref[q, rows, c0:c0 + PAIR_STATE]
                bi = bu_ref[q, rows, c0 + PAIR_STATE:c0 + PAIR_COLS]
                xr, xi = a_re * xr - a_im * xi + br, a_re * xi + a_im * xr + bi
                xs_ref[q, rows, c0:c0 + PAIR_STATE] = xr
                xs_ref[q, rows, c0 + PAIR_STATE:c0 + PAIR_COLS] = xi
            state_ref[gp, 0] = xr
            state_ref[gp, 1] = xi

        y = jnp.dot(xs_ref[q].astype(BF16), cp_ref[q], preferred_element_type=F32)
        y = _gelu(y + d_ref[:, lanes] * u_tiles[q])
        for t in range(steps):
            yslab_ref[q, t * y_pitch:t * y_pitch + batch, :] = y[t * batch:(t + 1) * batch]
        for bi in range(batch):
            y_ref[bi, :, lanes] = yslab_ref[q, pl.ds(bi, steps, stride=y_pitch), :].astype(BF16)

    return [prepare] + [functools.partial(finish, q) for q in range(n_tiles)]


def _ssm_scratch(batch, width, n_pairs):
    rows = SSM_STEPS * batch
    n_tiles = width // LANES
    return [pltpu.VMEM((n_pairs, 2, batch, PAIR_STATE), F32),
            pltpu.VMEM((n_tiles, rows, PAIRS_PER_TILE * PAIR_COLS), F32),
            pltpu.VMEM((n_tiles, rows, PAIRS_PER_TILE * PAIR_COLS), F32),
            pltpu.VMEM((n_tiles, SSM_STEPS * _odd_pitch(batch), LANES), F32),
            pltpu.VMEM((n_tiles, batch * _odd_pitch(SSM_STEPS), LANES), F32)]


def _ssm_params(lam_re, lam_im, log_dt, b_re, b_im, c_re, c_im):
    g = lam_re.shape[0]
    dt = jnp.exp(log_dt.astype(F32))[:, None]
    lr = lam_re.astype(F32)
    li = lam_im.astype(F32)
    mag = jnp.exp(lr * dt)
    ar = mag * jnp.cos(li * dt)
    ai = mag * jnp.sin(li * dt)
    den = lr * lr + li * li
    cr = ((ar - 1.0) * lr + ai * li) / den
    ci = (ai * lr - (ar - 1.0) * li) / den
    br = b_re.astype(F32)
    bi = b_im.astype(F32)
    bbar_re = cr[..., None] * br - ci[..., None] * bi
    bbar_im = cr[..., None] * bi + ci[..., None] * br

    n_pairs = g // 2
    n_tiles = n_pairs // PAIRS_PER_TILE
    eye_pt = jnp.eye(PAIRS_PER_TILE, dtype=F32)
    eye_2 = jnp.eye(2, dtype=F32)
    bb = jnp.stack([bbar_re, bbar_im], axis=0).reshape(2, n_tiles, PAIRS_PER_TILE, 2, SSM_STATE, SSM_GROUP)
    bpad = jnp.einsum('aqlgpc,lm,gh->qlmgcahp', bb, eye_pt, eye_2)
    bpad = bpad.reshape(n_pairs, LANES, PAIR_COLS)
    cc = jnp.stack([c_re.astype(F32), -c_im.astype(F32)], axis=0)
    cc = cc.reshape(2, n_tiles, PAIRS_PER_TILE, 2, SSM_GROUP, SSM_STATE)
    cpad = jnp.einsum('aqlgcp,lm,gh->qlagpmhc', cc, eye_pt, eye_2)
    cpad = cpad.reshape(n_tiles, PAIRS_PER_TILE * PAIR_COLS, LANES)
    ar_v = ar.reshape(n_pairs, 1, PAIR_STATE)
    ai_v = ai.reshape(n_pairs, 1, PAIR_STATE)
    return bpad.astype(BF16), ar_v, ai_v, cpad.astype(BF16)


def _alibi_slopes():
    i = np.arange(1, N_ATTN_HEADS + 1, dtype=np.float32)
    s = np.exp2(-8.0 * i / N_ATTN_HEADS).astype(np.float32)
    return s.reshape(HEADS_PER_PATTERN, N_PATTERNS).T


def _attn_bias():
    slopes = _alibi_slopes()
    qi = np.arange(BLOCK)[:, None]
    ki = np.arange(2 * BLOCK)[None, :]
    delta = qi + BLOCK - ki
    heads = LANES // HEAD_DIM
    out = np.empty((N_PATTERNS, HEADS_PER_PATTERN // heads, heads * BLOCK, 2 * BLOCK), np.float32)
    for g, (window, dilation) in enumerate(ATTN_PATTERNS):
        reach = window // dilation
        valid = (delta >= 0) & (delta <= reach)
        dist = (delta * dilation).astype(np.float32)
        for h in range(HEADS_PER_PATTERN):
            rows = slice((h % heads) * BLOCK, (h % heads + 1) * BLOCK)
            out[g, h // heads, rows] = np.where(valid, -slopes[g, h] * dist * LOG2E, -np.inf)
    return out


def _attn_ssm_kernel(qkv0, qkv1, qkv2, bias_ref, u_ref, bp_ref, ar_ref, ai_ref, cp_ref, d_ref, o_ref, y_ref,
                     og0, og1, og2, dg0, dg1, dg2, mg0, mg1, mg2, *ssm_scratch):
    step_index = pl.program_id(0) * pl.num_programs(1) + pl.program_id(1)
    ssm_stages = _ssm_stages(step_index == 0, u_ref, bp_ref, ar_ref, ai_ref, cp_ref, d_ref, y_ref, *ssm_scratch)
    qkv = (qkv0, qkv1, qkv2)
    ogs = (og0, og1, og2)
    dgs = (dg0, dg1, dg2)
    mgs = (mg0, mg1, mg2)
    seq = qkv0.shape[0]
    n_blocks = seq // BLOCK
    q_cols, k_cols, v_cols = (slice(i * LANES, (i + 1) * LANES) for i in range(3))
    lane = lax.broadcasted_iota(jnp.int32, (BLOCK, LANES), 1)
    head0 = lane < HEAD_DIM

    blocks = [(g, bi) for g in range(N_PATTERNS) for bi in range(n_blocks)]

    def is_first(g, bi):
        return bi % (n_blocks // ATTN_PATTERNS[g][1]) == 0

    def split_heads(a):
        lanes = lax.broadcasted_iota(jnp.int32, a.shape, 1)
        keep = (lanes & (LANES - 1)) < HEAD_DIM
        zero = jnp.zeros_like(a)
        return jnp.concatenate([jnp.where(keep, a, zero), jnp.where(keep, zero, a)], axis=0)

    def logits(g, bi):
        ref = qkv[g]
        q = ref[bi * BLOCK:(bi + 1) * BLOCK, q_cols]
        contract_lanes = (((1,), (1,)), ((), ()))
        if is_first(g, bi):
            k = split_heads(ref[bi * BLOCK:(bi + 1) * BLOCK, k_cols])
            s = lax.dot_general(q, k, contract_lanes, preferred_element_type=F32)
            bias = jnp.concatenate([bias_ref[g, :BLOCK, BLOCK:], bias_ref[g, BLOCK:, BLOCK:]], axis=1)
            return s + bias
        k = ref[(bi - 1) * BLOCK:(bi + 1) * BLOCK, k_cols]
        s = lax.dot_general(split_heads(q), k, contract_lanes, preferred_element_type=F32)
        return s + bias_ref[g]

    def softmax_numerator(g, bi, s):
        if is_first(g, bi):
            m0 = jnp.max(s[:, :BLOCK], axis=-1, keepdims=True)
            m1 = jnp.max(s[:, BLOCK:], axis=-1, keepdims=True)
            p = jnp.concatenate([jnp.exp2(s[:, :BLOCK] - m0), jnp.exp2(s[:, BLOCK:] - m1)], axis=1)
            return p.astype(BF16), (m0, m1)
        m = jnp.max(s, axis=-1, keepdims=True)
        return jnp.exp2(s - m).astype(BF16), (m[:BLOCK], m[BLOCK:])

    def weighted_values(g, bi, p, m):
        ref = qkv[g]
        dilation = ATTN_PATTERNS[g][1]
        nb = n_blocks // dilation
        out_start = bi // nb + (bi % nb) * BLOCK * dilation
        dst = pl.ds(out_start, BLOCK) if dilation == 1 else pl.ds(out_start, BLOCK, stride=dilation)
        if is_first(g, bi):
            v = ref[bi * BLOCK:(bi + 1) * BLOCK, v_cols]
            pv = jnp.dot(p, split_heads(jnp.concatenate([v, jnp.ones_like(v)], axis=1)),
                         preferred_element_type=F32)
            ogs[g][dst, :] = pv[:, :LANES]
            dgs[g][dst, :] = pv[:, LANES:]
        else:
            v = ref[(bi - 1) * BLOCK:(bi + 1) * BLOCK, v_cols]
            pv = jnp.dot(p, jnp.concatenate([v, jnp.ones_like(v)], axis=1), preferred_element_type=F32)
            ogs[g][dst, :] = jnp.where(head0, pv[:BLOCK, :LANES], pv[BLOCK:, :LANES])
            dgs[g][dst, :] = jnp.where(head0, pv[:BLOCK, LANES:], pv[BLOCK:, LANES:])
        mgs[g][dst, :] = jnp.where(head0, jnp.broadcast_to(m[0], (BLOCK, LANES)),
                                   jnp.broadcast_to(m[1], (BLOCK, LANES)))

    scores, numer = {}, {}
    n_steps = len(blocks) + 2 * ATTN_LAG
    ssm_at = {(k * n_steps) // len(ssm_stages): stage for k, stage in enumerate(ssm_stages)}
    for step in range(n_steps):
        if step in ssm_at:
            ssm_at[step]()
        if step < len(blocks):
            scores[step] = logits(*blocks[step])
        n = step - ATTN_LAG
        if 0 <= n < len(blocks):
            numer[n] = softmax_numerator(*blocks[n], scores.pop(n))
        n = step - 2 * ATTN_LAG
        if 0 <= n < len(blocks):
            weighted_values(*blocks[n], *numer.pop(n))

    chunk = 2 * BLOCK
    for c in range(0, seq, chunk):
        rows = pl.ds(c, chunk)
        ms = [mg[rows, :] for mg in mgs]
        mx = jnp.maximum(jnp.maximum(ms[0], ms[1]), ms[2])
        es = [jnp.exp2(m - mx) for m in ms]
        tot = es[0] * dgs[0][rows, :] + es[1] * dgs[1][rows, :] + es[2] * dgs[2][rows, :]
        acc = es[0] * ogs[0][rows, :] + es[1] * ogs[1][rows, :] + es[2] * ogs[2][rows, :]
        o_ref[rows, :] = (acc * (1.0 / tot)).astype(BF16)


def _attention_and_ssm(qkv, bias, u, bpad, ar_v, ai_v, cpad, d_skip):
    b, col_blocks, seq, _ = qkv[0].shape
    width = u.shape[2]
    assert col_blocks * b == seq // SSM_STEPS, "one SSM time block per attention grid step"
    window = pl.BlockSpec((None, None, seq, 3 * LANES), lambda j, i: (i, j, 0, 0))
    bias_spec = pl.BlockSpec((N_PATTERNS, None) + bias.shape[2:], lambda j, i: (0, j, 0, 0))
    time_block = pl.BlockSpec((b, SSM_STEPS, width), lambda j, i: (0, j * b + i, 0))
    ssm_weights = (bpad, ar_v, ai_v, cpad, d_skip)
    return pl.pallas_call(
        _attn_ssm_kernel,
        grid=(col_blocks, b),
        in_specs=[window] * N_PATTERNS + [bias_spec, time_block] + [_resident(w.shape) for w in ssm_weights],
        out_specs=[pl.BlockSpec((None, seq, LANES), lambda j, i: (i, 0, j)), time_block],
        out_shape=[jax.ShapeDtypeStruct((b, seq, col_blocks * LANES), BF16),
                   jax.ShapeDtypeStruct((b, seq, width), BF16)],
        scratch_shapes=[pltpu.VMEM((seq, LANES), F32)] * 9 + _ssm_scratch(b, width, bpad.shape[0]),
        compiler_params=pltpu.CompilerParams(
            dimension_semantics=("arbitrary", "arbitrary"), vmem_limit_bytes=VMEM_LIMIT_BYTES),
        name="attn_ssm",
    )(*qkv, bias, u, *ssm_weights)


def _tail_kernel(x_ref, ys_ref, oa_ref, gates_ref, wv_ref, wg_ref, wup_ref, wout_ref,
                 fg_ref, fwg_ref, fwu_ref, cw_ref, cb_ref, fwd_ref, o_ref, carry_ref, gs_ref, h_ref):
    rows, d = x_ref.shape
    halo = SUBLANES

    @pl.when(pl.program_id(1) == 0)
    def _():
        carry_ref[...] = jnp.zeros_like(carry_ref)

    ys = ys_ref[...]
    val = jnp.dot(ys, wv_ref[...], preferred_element_type=F32)
    gate = jnp.dot(ys, wg_ref[...], preferred_element_type=F32)
    y_a = val * jax.nn.sigmoid(gate)
    y_b = jnp.dot(oa_ref[...], wup_ref[...], preferred_element_type=F32)
    merged = gates_ref[:, :d].astype(F32) * y_a + gates_ref[:, d:].astype(F32) * y_b
    x = x_ref[...] + jnp.dot(merged.astype(BF16), wout_ref[...], preferred_element_type=F32)

    ms = jnp.mean(x * x, axis=-1, keepdims=True)
    hf = (x * lax.rsqrt(ms + EPS) * fg_ref[...]).astype(BF16)
    d_ff = fwg_ref.shape[1]
    for c in range(0, d_ff, FFN_COLS):
        cols = slice(c, c + FFN_COLS)
        gate = jnp.dot(hf, fwg_ref[:, cols], preferred_element_type=F32)
        up = jnp.dot(hf, fwu_ref[:, cols], preferred_element_type=F32)
        gs_ref[0:halo, :] = carry_ref[:, cols]
        gs_ref[halo:halo + rows, :] = gate
        carry_ref[:, cols] = gate[rows - halo:rows, :]
        pre = cb_ref[:, cols] + cw_ref[CONV_WIDTH - 1:CONV_WIDTH, cols] * gate
        for tap in range(1, CONV_WIDTH):
            shifted = gs_ref[halo - tap:halo - tap + rows, :]
            pre = pre + cw_ref[CONV_WIDTH - 1 - tap:CONV_WIDTH - tap, cols] * shifted
        h_ref[:, cols] = (_gelu(pre) * up).astype(BF16)
    o_ref[...] = x + jnp.dot(h_ref[...], fwd_ref[...], preferred_element_type=F32)


def _tail(x, ys, oa, gates, wv, wg, wup, wout, norm_g, fwg, fwu, conv_w, conv_b, fwd):
    b, seq, d = x.shape
    d_ff = fwg.shape[1]
    tile = lambda width: pl.BlockSpec((None, TAIL_ROWS, width), lambda i, j: (i, j, 0))
    weights = (wv, wg, wup, wout, norm_g, fwg, fwu, conv_w, conv_b, fwd)
    return pl.pallas_call(
        _tail_kernel,
        grid=(b, seq // TAIL_ROWS),
        in_specs=[tile(d), tile(ys.shape[2]), tile(oa.shape[2]), tile(gates.shape[2])]
                 + [_resident(w.shape) for w in weights],
        out_specs=tile(d),
        out_shape=jax.ShapeDtypeStruct((b, seq, d), F32),
        scratch_shapes=[pltpu.VMEM((SUBLANES, d_ff), F32),
                        pltpu.VMEM((TAIL_ROWS + SUBLANES, FFN_COLS), F32),
                        pltpu.VMEM((TAIL_ROWS, d_ff), BF16)],
        compiler_params=pltpu.CompilerParams(
            dimension_semantics=("arbitrary", "arbitrary"), vmem_limit_bytes=VMEM_LIMIT_BYTES),
        name="tail",
    )(x, ys, oa, gates, *weights)


def _layer(x, norm_mix_g, w_in, lam_re, lam_im, log_dt, b_re, b_im, c_re, c_im, ssm_d,
           glu_w_val, glu_w_gate, q_norm_g, k_norm_g, w_attn_up, w_out, norm_ffn_g,
           ffn_w_gate, ffn_w_up, ffn_conv_w, ffn_conv_b, ffn_w_down):
    b, seq, d = x.shape
    scale = HEAD_DIM ** -0.5 * LOG2E
    q_gain = jnp.tile(q_norm_g.astype(F32) * scale, HEADS_PER_PATTERN)[None, :]
    k_gain = jnp.tile(k_norm_g.astype(F32), HEADS_PER_PATTERN)[None, :]

    outs = _in_proj(x, norm_mix_g.astype(F32)[None, :], w_in.astype(BF16), q_gain, k_gain)
    u, qkv, gates = outs[0], outs[1:1 + N_PATTERNS], outs[1 + N_PATTERNS]

    bpad, ar_v, ai_v, cpad = _ssm_params(lam_re, lam_im, log_dt, b_re, b_im, c_re, c_im)
    attn_in = [a.reshape(b, PATTERN_WIDTH // LANES, seq, 3 * LANES) for a in qkv]
    o_attn, ys = _attention_and_ssm(attn_in, jnp.asarray(_attn_bias()), u, bpad, ar_v, ai_v, cpad,
                                    ssm_d.astype(F32)[None, :])

    out = _tail(x, ys, o_attn, gates, glu_w_val.astype(BF16),
                glu_w_gate.astype(BF16), w_attn_up.astype(BF16), w_out.astype(BF16),
                norm_ffn_g.astype(F32)[None, :], ffn_w_gate.astype(BF16), ffn_w_up.astype(BF16),
                ffn_conv_w.astype(F32), ffn_conv_b.astype(F32)[None, :], ffn_w_down.astype(BF16))
    return out


def kernel(x, norm_mix_g, w_in, ssm_lambda_re, ssm_lambda_im, ssm_log_dt, ssm_b_re, ssm_b_im, ssm_c_re, ssm_c_im, ssm_d, glu_w_val, glu_w_gate, q_norm_g, k_norm_g, w_attn_up, w_out, norm_ffn_g, ffn_w_gate, ffn_w_up, ffn_conv_w, ffn_conv_b, ffn_w_down):
    depth = w_in.shape[0]
    for i in range(depth):
        x = _layer(x, norm_mix_g[i], w_in[i], ssm_lambda_re[i], ssm_lambda_im[i], ssm_log_dt[i],
                   ssm_b_re[i], ssm_b_im[i], ssm_c_re[i], ssm_c_im[i], ssm_d[i], glu_w_val[i],
                   glu_w_gate[i], q_norm_g[i], k_norm_g[i], w_attn_up[i], w_out[i], norm_ffn_g[i],
                   ffn_w_gate[i], ffn_w_up[i], ffn_conv_w[i], ffn_conv_b[i], ffn_w_down[i]).astype(x.dtype)
    return x
```
